```python
import jax, jax.numpy as jnp
from jax import lax
import numpy as np

D_MODEL = 1024
BATCH = 16
SEQ = 4096
DEPTH = 4
DEC_BATCH = 32
DEC_SEQ = 64
PAST_LEN = 2048

CHUNK = 64
N_A_LAYERS = DEPTH // 2
N_B_LAYERS = DEPTH - N_A_LAYERS
TOK_WIDTH = D_MODEL // 2
MEM_WIDTH = D_MODEL // 2
A_CHUNK = 128
A_GROUPS = 8
A_GROUP_DIM = TOK_WIDTH // A_GROUPS
B_HEADS = 8
B_HEAD_DIM = TOK_WIDTH // B_HEADS
B_LEFT_CHUNKS = 8
B_WINDOW = B_LEFT_CHUNKS * CHUNK
REL_CLIP = 256
N_REL = 2 * REL_CLIP + 1
CACHE_B_LEN = min(B_WINDOW, PAST_LEN)
N_MEM = 256
MEM_HEADS = 4
MEM_HEAD_DIM = MEM_WIDTH // MEM_HEADS
D_FF = 128 * ((8 * D_MODEL // 3 + 127) // 128)
CONV_W = 3
EPS = 1e-6
NEG_INF = -1e30

kernel_name = 'yoco_gmlp_chunkband_streaming_step'


def rmsnorm(x, g):
    xf = x.astype(jnp.float32)
    y = xf * lax.rsqrt(jnp.mean(xf * xf, axis=-1, keepdims=True) + EPS)
    return (y * g.astype(jnp.float32)).astype(x.dtype)


def spatial_gate(u, v, w_s, b_s):
    bsz, L, _ = v.shape
    n_chunks = -(-L // A_CHUNK)
    pad = n_chunks * A_CHUNK - L
    vp = jnp.pad(v, ((0, 0), (0, pad), (0, 0))).reshape(bsz, n_chunks, A_CHUNK, A_GROUPS, A_GROUP_DIM)
    mask = jnp.tril(jnp.ones((A_CHUNK, A_CHUNK), dtype=bool))
    w = jnp.where(mask[None], w_s, jnp.zeros_like(w_s))
    s = jnp.einsum('gij,bcjgd->bcigd', w, vp) + b_s.T[None, None, :, :, None]
    s = s.reshape(bsz, n_chunks * A_CHUNK, TOK_WIDTH)[:, :L]
    return u * s


def mem_attend(q, k, v):
    bsz, L, _ = q.shape
    q = q.reshape(bsz, L, MEM_HEADS, MEM_HEAD_DIM)
    s = jnp.einsum('blhd,bmhd->bhlm', q, k).astype(jnp.float32) * (MEM_HEAD_DIM ** -0.5)
    p = jax.nn.softmax(s, axis=-1).astype(v.dtype)
    return jnp.einsum('bhlm,bmhd->blhd', p, v).reshape(bsz, L, MEM_WIDTH)


def band_attend(q, k, v, q_pos, k_pos, rel_bias):
    s = jnp.einsum('bqhd,bkhd->bhqk', q, k).astype(jnp.float32) * (B_HEAD_DIM ** -0.5)
    rel = jnp.clip(q_pos[:, None] - k_pos[None, :], -REL_CLIP, REL_CLIP) + REL_CLIP
    bias = rel_bias.astype(jnp.float32)[:, rel]
    q_chunk = q_pos[:, None] // CHUNK
    valid = ((k_pos[None, :] >= 0)
             & (k_pos[None, :] < (q_chunk + 1) * CHUNK)
             & (k_pos[None, :] >= (q_chunk - B_LEFT_CHUNKS) * CHUNK))
    s = jnp.where(valid[None, None], s + bias[None], NEG_INF)
    p = jax.nn.softmax(s, axis=-1).astype(v.dtype)
    return jnp.einsum('bhqk,bkhd->bqhd', p, v)


def band_attention_prompt(q, k, v, rel_bias):
    bsz, S = q.shape[0], q.shape[1]
    n_chunks = S // CHUNK
    band = B_WINDOW + CHUNK
    kp = jnp.pad(k, ((0, 0), (B_WINDOW, 0), (0, 0), (0, 0)))
    vp = jnp.pad(v, ((0, 0), (B_WINDOW, 0), (0, 0), (0, 0)))
    qc = q.reshape(bsz, n_chunks, CHUNK, B_HEADS, B_HEAD_DIM).transpose(1, 0, 2, 3, 4)

    def one_chunk(args):
        c, qb = args
        start = c * CHUNK
        kb = lax.dynamic_slice_in_dim(kp, start, band, axis=1)
        vb = lax.dynamic_slice_in_dim(vp, start, band, axis=1)
        q_pos = start + jnp.arange(CHUNK, dtype=jnp.int32)
        k_pos = start - B_WINDOW + jnp.arange(band, dtype=jnp.int32)
        return band_attend(qb, kb, vb, q_pos, k_pos, rel_bias)

    out = lax.map(one_chunk, (jnp.arange(n_chunks, dtype=jnp.int32), qc))
    return out.transpose(1, 0, 2, 3, 4).reshape(bsz, S, B_HEADS, B_HEAD_DIM)


def band_attention_sample(q, k_new, v_new, cache_k, cache_v, rel_bias):
    L = q.shape[1]
    C = cache_k.shape[1]
    kb = jnp.concatenate([cache_k, k_new], axis=1)
    vb = jnp.concatenate([cache_v, v_new], axis=1)
    q_pos = PAST_LEN + jnp.arange(L, dtype=jnp.int32)
    k_pos = PAST_LEN - C + jnp.arange(C + L, dtype=jnp.int32)
    return band_attend(q, kb, vb, q_pos, k_pos, rel_bias)


def causal_dwconv(a, prev, w, b):
    L = a.shape[1]
    ap = jnp.concatenate([prev, a], axis=1)
    y = b + w[0] * ap[:, 0:L]
    for i in range(1, CONV_W):
        y = y + w[i] * ap[:, i:i + L]
    return y, ap[:, L:]


def trunk(h, is_prompt, mem_k, mem_v, conv_prev, b_cache_k, b_cache_v,
          g_mix, w_in_a, g_v, w_s, b_s, g_kv, w_kv, w_in_b, rel_bias,
          w_out, g_ffn, w_up, conv_w, conv_b, w_down, g_final):
    bsz, L, _ = h.shape
    a_v_rows, conv_states = [], []
    k_sh = v_sh = None
    for l in range(DEPTH):
        xn = rmsnorm(h, g_mix[l])
        if l < N_A_LAYERS:
            z = xn @ w_in_a[l]
            u, v = jnp.split(jax.nn.gelu(z[..., :2 * TOK_WIDTH]), 2, axis=-1)
            v = rmsnorm(v, g_v[l])
            if not is_prompt:
                a_v_rows.append(v)
            tok = spatial_gate(u, v, w_s[l], b_s[l])
            q_mem = z[..., 2 * TOK_WIDTH:]
        else:
            if l == N_A_LAYERS:
                kv = rmsnorm(h, g_kv) @ w_kv
                k_sh = kv[..., :TOK_WIDTH].reshape(bsz, L, B_HEADS, B_HEAD_DIM)
                v_sh = kv[..., TOK_WIDTH:].reshape(bsz, L, B_HEADS, B_HEAD_DIM)
            j = l - N_A_LAYERS
            z = xn @ w_in_b[j]
            q = z[..., :TOK_WIDTH].reshape(bsz, L, B_HEADS, B_HEAD_DIM)
            if is_prompt:
                o = band_attention_prompt(q, k_sh, v_sh, rel_bias[j])
            else:
                o = band_attention_sample(q, k_sh, v_sh, b_cache_k, b_cache_v, rel_bias[j])
            tok = o.reshape(bsz, L, TOK_WIDTH)
            q_mem = z[..., TOK_WIDTH:]
        mo = mem_attend(q_mem, mem_k[l], mem_v[l])
        h = h + jnp.concatenate([tok, mo], axis=-1) @ w_out[l]
        up = rmsnorm(h, g_ffn[l]) @ w_up[l]
        a, gate = jnp.split(up, 2, axis=-1)
        a_c, cs = causal_dwconv(a, conv_prev[l], conv_w[l], conv_b[l])
        conv_states.append(cs)
        h = h + (jax.nn.gelu(a_c) * gate) @ w_down[l]
    a_v = jnp.stack(a_v_rows) if a_v_rows else None
    return rmsnorm(h, g_final), a_v, k_sh, v_sh, jnp.stack(conv_states)


def setup_inputs(seed: int = 0) -> dict:
    key = jax.random.key(seed)
    ks = iter(jax.random.split(key, 32))

    def nrm(shape, scale=1.0):
        return jax.random.normal(next(ks), shape, jnp.float32) * scale

    def gain(shape):
        return 1.0 + 0.05 * nrm(shape)

    return {
        'x_prompt': nrm((BATCH, SEQ, D_MODEL)),
        'x_sample': nrm((DEC_BATCH, DEC_SEQ, D_MODEL)),
        'mem_prompt': nrm((BATCH, N_MEM, D_MODEL)),
        'cache_mem_k': nrm((DEPTH, DEC_BATCH, N_MEM, MEM_HEADS, MEM_HEAD_DIM)),
        'cache_mem_v': nrm((DEPTH, DEC_BATCH, N_MEM, MEM_HEADS, MEM_HEAD_DIM)),
        'cache_b_k': nrm((DEC_BATCH, CACHE_B_LEN, B_HEADS, B_HEAD_DIM)),
        'cache_b_v': nrm((DEC_BATCH, CACHE_B_LEN, B_HEADS, B_HEAD_DIM)),
        'state_conv': nrm((DEPTH, DEC_BATCH, CONV_W - 1, D_FF)),
        'g_mix': gain((DEPTH, D_MODEL)),
        'w_in_a': nrm((N_A_LAYERS, D_MODEL, 2 * TOK_WIDTH + MEM_WIDTH), D_MODEL ** -0.5),
        'g_v': gain((N_A_LAYERS, TOK_WIDTH)),
        'w_s': nrm((N_A_LAYERS, A_GROUPS, A_CHUNK, A_CHUNK), A_CHUNK ** -0.5),
        'b_s': 1.0 + 0.1 * nrm((N_A_LAYERS, A_GROUPS, A_CHUNK)),
        'g_kv': gain((D_MODEL,)),
        'w_kv': nrm((D_MODEL, 2 * TOK_WIDTH), D_MODEL ** -0.5),
        'w_in_b': nrm((N_B_LAYERS, D_MODEL, TOK_WIDTH + MEM_WIDTH), D_MODEL ** -0.5),
        'rel_bias': nrm((N_B_LAYERS, B_HEADS, N_REL), 0.5),
        'g_mem': gain((DEPTH, D_MODEL)),
        'w_mem_kv': nrm((DEPTH, D_MODEL, 2 * MEM_WIDTH), D_MODEL ** -0.5),
        'w_out': nrm((DEPTH, TOK_WIDTH + MEM_WIDTH, D_MODEL), (TOK_WIDTH + MEM_WIDTH) ** -0.5),
        'g_ffn': gain((DEPTH, D_MODEL)),
        'w_up': nrm((DEPTH, D_MODEL, 2 * D_FF), D_MODEL ** -0.5),
        'conv_w': nrm((DEPTH, CONV_W, D_FF), CONV_W ** -0.5),
        'conv_b': nrm((DEPTH, D_FF), 0.01),
        'w_down': nrm((DEPTH, D_FF, D_MODEL), D_FF ** -0.5),
        'g_final': gain((D_MODEL,)),
    }


def reference(x_prompt, x_sample, mem_prompt, cache_mem_k, cache_mem_v, cache_b_k, cache_b_v, state_conv,
              g_mix, w_in_a, g_v, w_s, b_s, g_kv, w_kv, w_in_b, rel_bias, g_mem, w_mem_kv,
              w_out, g_ffn, w_up, conv_w, conv_b, w_down, g_final):
    bsz = mem_prompt.shape[0]
    mem_n = rmsnorm(mem_prompt[None], g_mem[:, None, None, :])
    mkv = jnp.einsum('lbnd,lde->lbne', mem_n, w_mem_kv)
    mem_k_p = mkv[..., :MEM_WIDTH].reshape(DEPTH, bsz, N_MEM, MEM_HEADS, MEM_HEAD_DIM)
    mem_v_p = mkv[..., MEM_WIDTH:].reshape(DEPTH, bsz, N_MEM, MEM_HEADS, MEM_HEAD_DIM)
    conv0 = jnp.zeros((DEPTH, bsz, CONV_W - 1, D_FF), x_prompt.dtype)

    y_p, _, k_p, v_p, conv_p = trunk(
        x_prompt, True, mem_k_p, mem_v_p, conv0, None, None,
        g_mix, w_in_a, g_v, w_s, b_s, g_kv, w_kv, w_in_b, rel_bias,
        w_out, g_ffn, w_up, conv_w, conv_b, w_down, g_final)
    y_s, a_v_s, k_s, v_s, conv_s = trunk(
        x_sample, False, cache_mem_k, cache_mem_v, state_conv, cache_b_k, cache_b_v,
        g_mix, w_in_a, g_v, w_s, b_s, g_kv, w_kv, w_in_b, rel_bias,
        w_out, g_ffn, w_up, conv_w, conv_b, w_down, g_final)

    return (y_p, y_s, mem_k_p, mem_v_p, k_p[:, -B_WINDOW:], v_p[:, -B_WINDOW:], conv_p,
            a_v_s, k_s, v_s, conv_s)
```

```python
import functools

import numpy as np
import jax
import jax.numpy as jnp
from jax import lax
from jax.experimental import pallas as pl
from jax.experimental.pallas import tpu as pltpu

D_MODEL = 1024
DEPTH = 4
N_A_LAYERS = 2
TOK_WIDTH = 512
MEM_WIDTH = 512
A_CHUNK = 128
A_GROUPS = 8
B_HEADS = 8
B_HEAD_DIM = 64
CHUNK = 64
B_LEFT_CHUNKS = 8
B_WINDOW = B_LEFT_CHUNKS * CHUNK
REL_CLIP = 256
N_MEM = 256
MEM_HEADS = 4
MEM_HEAD_DIM = 128
D_FF = 2816
CONV_W = 3
EPS = 1e-6
NEG_INF = -1e30

LANES = 128
PROMPT_TILE = 512
FF_CHUNK = 256
N_FF_CHUNKS = D_FF // FF_CHUNK
Q_BLOCK = 256
VMEM_LIMIT = 56 * 1024 * 1024

F32 = jnp.float32
BF16 = jnp.bfloat16


def _rms(x, g):
    ms = jnp.mean(x * x, axis=-1, keepdims=True)
    return x * lax.rsqrt(ms + EPS) * g


def _softmax_parts(s):
    m = jnp.max(s, axis=-1, keepdims=True)
    p = jnp.exp(s - m)
    return p, jnp.sum(p, axis=-1, keepdims=True)


def _mem_attend(q_mem, mkT_ref, mv_ref):
    outs = []
    for hh in range(MEM_HEADS):
        sl = slice(hh * MEM_HEAD_DIM, (hh + 1) * MEM_HEAD_DIM)
        q = (q_mem[:, sl] * (MEM_HEAD_DIM ** -0.5)).astype(BF16)
        s = jnp.dot(q, mkT_ref[0, sl, :], preferred_element_type=F32)
        p, l = _softmax_parts(s)
        o = jnp.dot(p.astype(BF16), mv_ref[0, :, sl], preferred_element_type=F32)
        outs.append(o * (1.0 / l))
    return jnp.concatenate(outs, axis=-1)


def _spatial_gate(u, v, ws_ref, bs_ref, chunk):
    rows = u.shape[0]
    r = lax.broadcasted_iota(jnp.int32, (A_CHUNK, A_CHUNK), 0)
    c = lax.broadcasted_iota(jnp.int32, (A_CHUNK, A_CHUNK), 1)
    w_tril = [jnp.where(r >= c, ws_ref[g], 0.0).astype(BF16)[:chunk, :] for g in range(A_GROUPS)]
    lane = lax.broadcasted_iota(jnp.int32, (A_CHUNK, LANES), 1)
    low = lane < (LANES // 2)
    bias = bs_ref[:chunk, :]
    toks = []
    for ci in range(rows // chunk):
        vc = v[ci * chunk:(ci + 1) * chunk]
        if chunk < A_CHUNK:
            vc = jnp.concatenate([vc, jnp.zeros((A_CHUNK - chunk, TOK_WIDTH), F32)], axis=0)
        parts = []
        for pr in range(A_GROUPS // 2):
            vp = vc[:, pr * LANES:(pr + 1) * LANES]
            v_lo = jnp.where(low, vp, 0.0).astype(BF16)
            v_hi = jnp.where(low, 0.0, vp).astype(BF16)
            parts.append(jnp.dot(w_tril[2 * pr], v_lo, preferred_element_type=F32)
                         + jnp.dot(w_tril[2 * pr + 1], v_hi, preferred_element_type=F32))
        s = jnp.concatenate(parts, axis=-1) + bias
        toks.append(u[ci * chunk:(ci + 1) * chunk] * s)
    return jnp.concatenate(toks, axis=0) if len(toks) > 1 else toks[0]


def _mixer_a_kernel(h_ref, gmix_ref, win_ref, gv_ref, ws_ref, bs_ref, mkT_ref, mv_ref, wout_ref,
                    *out_refs, chunk, emit_v):
    out_ref = out_refs[0]
    x = h_ref[0]
    xn = _rms(x, gmix_ref[...]).astype(BF16)
    z = jnp.dot(xn, win_ref[...], preferred_element_type=F32)
    uv = jax.nn.gelu(z[:, :2 * TOK_WIDTH])
    u = uv[:, :TOK_WIDTH]
    v = _rms(uv[:, TOK_WIDTH:], gv_ref[...])
    if emit_v:
        out_refs[1][0] = v
    tok = _spatial_gate(u, v, ws_ref, bs_ref, chunk)
    mo = _mem_attend(z[:, 2 * TOK_WIDTH:], mkT_ref, mv_ref)
    cat = jnp.concatenate([tok, mo], axis=-1).astype(BF16)
    out_ref[0] = x + jnp.dot(cat, wout_ref[...], preferred_element_type=F32)


def _band_attend(q, kT_win, v_win, bias_ref, blocks, first_tile):
    qb, kb = bias_ref.shape[1], bias_ref.shape[2]
    lane_q = lax.broadcasted_iota(jnp.int32, (qb, LANES), 1)
    low = lane_q < B_HEAD_DIM
    lane_k = lax.broadcasted_iota(jnp.int32, (1, kb), 1)
    qs = q * (B_HEAD_DIM ** -0.5)
    row_parts = []
    for (row0, key0, n_prev) in blocks:
        pm = None
        if n_prev:
            pm = jnp.where(lane_k < n_prev, jnp.where(first_tile, NEG_INF, 0.0), 0.0)
        pair_outs = []
        for pr in range(B_HEADS // 2):
            psl = slice(pr * LANES, (pr + 1) * LANES)
            qp = qs[row0:row0 + qb, psl]
            kT = kT_win[psl, key0:key0 + kb]
            vv = v_win[key0:key0 + kb, psl]
            halves = []
            for half in range(2):
                qh = (jnp.where(low, qp, 0.0) if half == 0 else jnp.where(low, 0.0, qp)).astype(BF16)
                s = jnp.dot(qh, kT, preferred_element_type=F32) + bias_ref[2 * pr + half]
                if pm is not None:
                    s = s + pm
                p, l = _softmax_parts(s)
                o = jnp.dot(p.astype(BF16), vv, preferred_element_type=F32)
                halves.append(o * (1.0 / l))
            pair_outs.append(jnp.where(low, halves[0], halves[1]))
        row_parts.append(jnp.concatenate(pair_outs, axis=-1))
    return jnp.concatenate(row_parts, axis=0) if len(row_parts) > 1 else row_parts[0]


def _mixer_b_kernel(h_ref, gmix_ref, win_ref, kTa_ref, kTb_ref, va_ref, vb_ref, bias_ref,
                    mkT_ref, mv_ref, wout_ref, out_ref, *, blocks):
    x = h_ref[0]
    xn = _rms(x, gmix_ref[...]).astype(BF16)
    z = jnp.dot(xn, win_ref[...], preferred_element_type=F32)
    kT_win = jnp.concatenate([kTa_ref[0], kTb_ref[0]], axis=1)
    v_win = jnp.concatenate([va_ref[0], vb_ref[0]], axis=0)
    tok = _band_attend(z[:, :TOK_WIDTH], kT_win, v_win, bias_ref, blocks, pl.program_id(1) == 0)
    mo = _mem_attend(z[:, TOK_WIDTH:], mkT_ref, mv_ref)
    cat = jnp.concatenate([tok, mo], axis=-1).astype(BF16)
    out_ref[0] = x + jnp.dot(cat, wout_ref[...], preferred_element_type=F32)


def _ffn_kernel(h_ref, g_ref, wa_ref, wg_ref, cw_ref, cb_ref, wd_ref, prev_ref, gfin_ref,
                out_ref, state_ref, carry_ref, acc_ref, *, final_norm):
    @pl.when(pl.program_id(1) == 0)
    def _():
        carry_ref[...] = prev_ref[0]

    x = h_ref[0]
    rows = x.shape[0]
    xn = _rms(x, g_ref[...]).astype(BF16)
    acc_ref[...] = x
    row_id = lax.broadcasted_iota(jnp.int32, (rows, FF_CHUNK), 0)

    def body(f, carry):
        a = jnp.dot(xn, wa_ref[f], preferred_element_type=F32)
        gate = jnp.dot(xn, wg_ref[f], preferred_element_type=F32)
        prev = carry_ref[f]
        p0, p1 = prev[0:1, :], prev[1:2, :]
        a1 = jnp.where(row_id == 0, p1, pltpu.roll(a, 1, 0))
        a2 = jnp.where(row_id == 0, p0, jnp.where(row_id == 1, p1, pltpu.roll(a, 2, 0)))
        cw = cw_ref[f]
        y = cb_ref[f] + cw[0:1, :] * a2 + cw[1:2, :] * a1 + cw[2:3, :] * a
        carry_ref[f] = a[rows - 2:rows, :]
        act = (jax.nn.gelu(y) * gate).astype(BF16)
        acc_ref[...] += jnp.dot(act, wd_ref[f], preferred_element_type=F32)
        return carry

    lax.fori_loop(0, N_FF_CHUNKS, body, 0)
    state_ref[0] = carry_ref[...]
    hn = acc_ref[...]
    out_ref[0] = _rms(hn, gfin_ref[...]) if final_norm else hn


def _kv_kernel(h_ref, g_ref, w_ref, kT_ref, v_ref, kf_ref, vf_ref):
    hn = _rms(h_ref[0], g_ref[...]).astype(BF16)
    kv = jnp.dot(hn, w_ref[...], preferred_element_type=F32)
    k, v = kv[:, :TOK_WIDTH], kv[:, TOK_WIDTH:]
    kf_ref[0] = k
    vf_ref[0] = v
    kT_ref[0] = k.T.astype(BF16)
    v_ref[0] = v.astype(BF16)


def _mem_kv_kernel(m_ref, g_ref, w_ref, k_ref, v_ref, kT_ref, vb_ref):
    mn = _rms(m_ref[0], g_ref[0]).astype(BF16)
    kv = jnp.dot(mn, w_ref[0], preferred_element_type=F32)
    k, v = kv[:, :MEM_WIDTH], kv[:, MEM_WIDTH:]
    k_ref[0, 0] = k
    v_ref[0, 0] = v
    kT_ref[0, 0] = k.T.astype(BF16)
    vb_ref[0, 0] = v.astype(BF16)


def _const_spec(shape):
    nd = len(shape)
    return pl.BlockSpec(shape, lambda b, i: (0,) * nd, pipeline_mode=pl.Buffered(1))


def _params():
    return pltpu.CompilerParams(dimension_semantics=("arbitrary", "arbitrary"),
                                vmem_limit_bytes=VMEM_LIMIT)


def _mixer_a(h, tile, gmix, win, gv, ws, bs_full, mkT, mv, wout, emit_v):
    nb, seq, _ = h.shape
    chunk = min(A_CHUNK, tile)
    act_spec = pl.BlockSpec((1, tile, D_MODEL), lambda b, i: (b, i, 0))
    out_shape = [jax.ShapeDtypeStruct(h.shape, F32)]
    out_specs = [act_spec]
    if emit_v:
        out_shape.append(jax.ShapeDtypeStruct((nb, seq, TOK_WIDTH), F32))
        out_specs.append(pl.BlockSpec((1, tile, TOK_WIDTH), lambda b, i: (b, i, 0)))
    res = pl.pallas_call(
        functools.partial(_mixer_a_kernel, chunk=chunk, emit_v=emit_v),
        grid=(nb, seq // tile),
        in_specs=[act_spec, _const_spec(gmix.shape), _const_spec(win.shape), _const_spec(gv.shape),
                  _const_spec(ws.shape), _const_spec(bs_full.shape),
                  pl.BlockSpec((1, MEM_WIDTH, N_MEM), lambda b, i: (b, 0, 0)),
                  pl.BlockSpec((1, N_MEM, MEM_WIDTH), lambda b, i: (b, 0, 0)),
                  _const_spec(wout.shape)],
        out_specs=out_specs, out_shape=out_shape, compiler_params=_params(),
        name="mixer_a",
    )(h, gmix, win, gv, ws, bs_full, mkT, mv, wout)
    return res if emit_v else (res[0], None)


def _mixer_b(h, tile, gmix, win, kT_a, kT_b, v_a, v_b, bias, blocks, mkT, mv, wout, prompt):
    nb, seq, _ = h.shape
    act_spec = pl.BlockSpec((1, tile, D_MODEL), lambda b, i: (b, i, 0))
    if prompt:
        ka = pl.BlockSpec((1, TOK_WIDTH, tile), lambda b, i: (b, 0, jnp.maximum(i - 1, 0)))
        kb = pl.BlockSpec((1, TOK_WIDTH, tile), lambda b, i: (b, 0, i))
        va = pl.BlockSpec((1, tile, TOK_WIDTH), lambda b, i: (b, jnp.maximum(i - 1, 0), 0))
        vb = pl.BlockSpec((1, tile, TOK_WIDTH), lambda b, i: (b, i, 0))
    else:
        ka = pl.BlockSpec((1,) + kT_a.shape[1:], lambda b, i: (b, 0, 0))
        kb = pl.BlockSpec((1,) + kT_b.shape[1:], lambda b, i: (b, 0, 0))
        va = pl.BlockSpec((1,) + v_a.shape[1:], lambda b, i: (b, 0, 0))
        vb = pl.BlockSpec((1,) + v_b.shape[1:], lambda b, i: (b, 0, 0))
    return pl.pallas_call(
        functools.partial(_mixer_b_kernel, blocks=blocks),
        grid=(nb, seq // tile),
        in_specs=[act_spec, _const_spec(gmix.shape), _const_spec(win.shape), ka, kb, va, vb,
                  _const_spec(bias.shape),
                  pl.BlockSpec((1, MEM_WIDTH, N_MEM), lambda b, i: (b, 0, 0)),
                  pl.BlockSpec((1, N_MEM, MEM_WIDTH), lambda b, i: (b, 0, 0)),
                  _const_spec(wout.shape)],
        out_specs=act_spec, out_shape=jax.ShapeDtypeStruct(h.shape, F32),
        compiler_params=_params(), name="mixer_b",
    )(h, gmix, win, kT_a, kT_b, v_a, v_b, bias, mkT, mv, wout)


def _ffn(h, tile, g, wa, wg, cw, cb, wd, prev, gfin, final_norm):
    nb, seq, _ = h.shape
    act_spec = pl.BlockSpec((1, tile, D_MODEL), lambda b, i: (b, i, 0))
    st_spec = pl.BlockSpec((1, N_FF_CHUNKS, CONV_W - 1, FF_CHUNK), lambda b, i: (b, 0, 0, 0))
    return pl.pallas_call(
        functools.partial(_ffn_kernel, final_norm=final_norm),
        grid=(nb, seq // tile),
        in_specs=[act_spec, _const_spec(g.shape), _const_spec(wa.shape), _const_spec(wg.shape),
                  _const_spec(cw.shape), _const_spec(cb.shape), _const_spec(wd.shape), st_spec,
                  _const_spec(gfin.shape)],
        out_specs=[act_spec, st_spec],
        out_shape=[jax.ShapeDtypeStruct(h.shape, F32),
                   jax.ShapeDtypeStruct((nb, N_FF_CHUNKS, CONV_W - 1, FF_CHUNK), F32)],
        scratch_shapes=[pltpu.VMEM((N_FF_CHUNKS, CONV_W - 1, FF_CHUNK), F32),
                        pltpu.VMEM((tile, D_MODEL), F32)],
        compiler_params=_params(), name="conv_ffn",
    )(h, g, wa, wg, cw, cb, wd, prev, gfin)


def _shared_kv(h, tile, g, w):
    nb, seq, _ = h.shape
    n_t = seq // tile
    act_spec = pl.BlockSpec((1, tile, D_MODEL), lambda b, i: (b, i, 0))
    last_spec = pl.BlockSpec((1, tile, TOK_WIDTH), lambda b, i: (b, 0, 0))
    return pl.pallas_call(
        _kv_kernel,
        grid=(nb, n_t),
        in_specs=[act_spec, _const_spec(g.shape), _const_spec(w.shape)],
        out_specs=[pl.BlockSpec((1, TOK_WIDTH, tile), lambda b, i: (b, 0, i)),
                   pl.BlockSpec((1, tile, TOK_WIDTH), lambda b, i: (b, i, 0)),
                   last_spec, last_spec],
        out_shape=[jax.ShapeDtypeStruct((nb, TOK_WIDTH, seq), BF16),
                   jax.ShapeDtypeStruct((nb, seq, TOK_WIDTH), BF16),
                   jax.ShapeDtypeStruct((nb, tile, TOK_WIDTH), F32),
                   jax.ShapeDtypeStruct((nb, tile, TOK_WIDTH), F32)],
        compiler_params=_params(), name="shared_kv",
    )(h, g, w)


def _mem_kv(mem, g, w):
    nb = mem.shape[0]
    f32_spec = pl.BlockSpec((1, 1, N_MEM, MEM_WIDTH), lambda l, b: (l, b, 0, 0))
    return pl.pallas_call(
        _mem_kv_kernel,
        grid=(DEPTH, nb),
        in_specs=[pl.BlockSpec((1, N_MEM, D_MODEL), lambda l, b: (b, 0, 0)),
                  pl.BlockSpec((1, 1, D_MODEL), lambda l, b: (l, 0, 0)),
                  pl.BlockSpec((1, D_MODEL, 2 * MEM_WIDTH), lambda l, b: (l, 0, 0))],
        out_specs=[f32_spec, f32_spec,
                   pl.BlockSpec((1, 1, MEM_WIDTH, N_MEM), lambda l, b: (l, b, 0, 0)),
                   f32_spec],
        out_shape=[jax.ShapeDtypeStruct((DEPTH, nb, N_MEM, MEM_WIDTH), F32),
                   jax.ShapeDtypeStruct((DEPTH, nb, N_MEM, MEM_WIDTH), F32),
                   jax.ShapeDtypeStruct((DEPTH, nb, MEM_WIDTH, N_MEM), BF16),
                   jax.ShapeDtypeStruct((DEPTH, nb, N_MEM, MEM_WIDTH), BF16)],
        compiler_params=_params(), name="mem_kv",
    )(mem, g, w)


def _band_bias(rel_bias, q_chunks, k_chunks):
    qi = np.arange(q_chunks * CHUNK)
    kj = np.arange(k_chunks * CHUNK)
    rel = B_WINDOW + qi[:, None] - kj[None, :]
    idx = np.clip(rel, -REL_CLIP, REL_CLIP) + REL_CLIP
    dc = kj[None, :] // CHUNK - qi[:, None] // CHUNK
    valid = (dc >= 0) & (dc <= B_LEFT_CHUNKS)
    return jnp.where(valid, rel_bias.astype(F32)[..., idx], NEG_INF)


def _chunk_cols(w, n_lead):
    shp = w.shape[:-1] + (N_FF_CHUNKS, FF_CHUNK)
    w = w.reshape(shp)
    nd = w.ndim
    perm = tuple(range(nd - 2 - n_lead)) + (nd - 2,) + tuple(range(nd - 2 - n_lead, nd - 2)) + (nd - 1,)
    return w.transpose(perm)


def _trunk(h, tile, prompt, mkT, mv, conv_prev, cache_kT, cache_v, wts):
    nb, seq, _ = h.shape
    a_v, states = [], []
    k_last = v_last = None
    for l in range(DEPTH):
        if l < N_A_LAYERS:
            h, v_rows = _mixer_a(h, tile, wts["g_mix"][l], wts["w_in_a"][l], wts["g_v"][l], wts["w_s"][l],
                                 wts["bs_full"][l], mkT[l], mv[l], wts["w_out"][l], emit_v=not prompt)
            if not prompt:
                a_v.append(v_rows)
        else:
            j = l - N_A_LAYERS
            if j == 0:
                kT, v_bf, k_last, v_last = _shared_kv(h, tile, wts["g_kv"], wts["w_kv"])
            if prompt:
                blocks = tuple((qb * Q_BLOCK, qb * Q_BLOCK, B_WINDOW - qb * Q_BLOCK)
                               for qb in range(tile // Q_BLOCK))
                h = _mixer_b(h, tile, wts["g_mix"][l], wts["w_in_b"][j], kT, kT, v_bf, v_bf,
                             wts["bias_prompt"][j], blocks, mkT[l], mv[l], wts["w_out"][l], True)
            else:
                h = _mixer_b(h, tile, wts["g_mix"][l], wts["w_in_b"][j], cache_kT, kT, cache_v, v_bf,
                             wts["bias_sample"][j], ((0, 0, 0),), mkT[l], mv[l], wts["w_out"][l], False)
        h, st = _ffn(h, tile, wts["g_ffn"][l], wts["w_up_a"][l], wts["w_up_g"][l], wts["conv_w"][l],
                     wts["conv_b"][l], wts["w_down"][l], conv_prev[l], wts["g_final"],
                     final_norm=(l == DEPTH - 1))
        states.append(st.transpose(0, 2, 1, 3).reshape(nb, CONV_W - 1, D_FF))
    return h, a_v, k_last, v_last, jnp.stack(states)


def kernel(x_prompt, x_sample, mem_prompt, cache_mem_k, cache_mem_v, cache_b_k, cache_b_v, state_conv,
           g_mix, w_in_a, g_v, w_s, b_s, g_kv, w_kv, w_in_b, rel_bias, g_mem, w_mem_kv,
           w_out, g_ffn, w_up, conv_w, conv_b, w_down, g_final):
    bsz = x_prompt.shape[0]
    dbs = x_sample.shape[0]
    dec_seq = x_sample.shape[1]
    wts = {
        "g_mix": g_mix.reshape(DEPTH, 1, D_MODEL),
        "w_in_a": w_in_a.astype(BF16),
        "g_v": g_v.reshape(N_A_LAYERS, 1, TOK_WIDTH),
        "w_s": w_s,
        "bs_full": jnp.repeat(b_s.transpose(0, 2, 1), TOK_WIDTH // A_GROUPS, axis=-1),
        "g_kv": g_kv.reshape(1, D_MODEL),
        "w_kv": w_kv.astype(BF16),
        "w_in_b": w_in_b.astype(BF16),
        "bias_prompt": _band_bias(rel_bias, Q_BLOCK // CHUNK, Q_BLOCK // CHUNK + B_LEFT_CHUNKS),
        "bias_sample": _band_bias(rel_bias, 1, 1 + B_LEFT_CHUNKS),
        "w_out": w_out.astype(BF16),
        "g_ffn": g_ffn.reshape(DEPTH, 1, D_MODEL),
        "w_up_a": _chunk_cols(w_up[..., :D_FF].astype(BF16), 1),
        "w_up_g": _chunk_cols(w_up[..., D_FF:].astype(BF16), 1),
        "conv_w": _chunk_cols(conv_w, 1),
        "conv_b": _chunk_cols(conv_b.reshape(DEPTH, 1, D_FF), 1),
        "w_down": w_down.astype(BF16).reshape(DEPTH, N_FF_CHUNKS, FF_CHUNK, D_MODEL),
        "g_final": g_final.reshape(1, D_MODEL),
    }

    mem_k_p, mem_v_p, mkT_p, mv_p = _mem_kv(mem_prompt, g_mem.reshape(DEPTH, 1, D_MODEL),
                                            w_mem_kv.astype(BF16))
    conv0 = jnp.zeros((DEPTH, bsz, N_FF_CHUNKS, CONV_W - 1, FF_CHUNK), F32)
    y_p, _, k_p, v_p, conv_p = _trunk(x_prompt, PROMPT_TILE, True, mkT_p, mv_p, conv0, None, None, wts)

    mkT_s = cache_mem_k.reshape(DEPTH, dbs, N_MEM, MEM_WIDTH).transpose(0, 1, 3, 2).astype(BF16)
    mv_s = cache_mem_v.reshape(DEPTH, dbs, N_MEM, MEM_WIDTH).astype(BF16)
    cache_kT = cache_b_k.reshape(dbs, -1, TOK_WIDTH).transpose(0, 2, 1).astype(BF16)
    cache_v = cache_b_v.reshape(dbs, -1, TOK_WIDTH).astype(BF16)
    conv_s0 = _chunk_cols(state_conv, 1)
    y_s, a_v_s, k_s, v_s, conv_s = _trunk(x_sample, dec_seq, False, mkT_s, mv_s, conv_s0,
                                          cache_kT, cache_v, wts)

    hd = (B_HEADS, B_HEAD_DIM)
    mh = (MEM_HEADS, MEM_HEAD_DIM)
    return (y_p, y_s,
            mem_k_p.reshape(DEPTH, bsz, N_MEM, *mh), mem_v_p.reshape(DEPTH, bsz, N_MEM, *mh),
            k_p.reshape(bsz, B_WINDOW, *hd), v_p.reshape(bsz, B_WINDOW, *hd),
            conv_p,
            jnp.stack(a_v_s),
            k_s.reshape(dbs, dec_seq, *hd), v_s.reshape(dbs, dec_seq, *hd),
            conv_s)
```

```python
import functools

import numpy as np
import jax
import jax.numpy as jnp
from jax import lax
from jax.experimental import pallas as pl
from jax.experimental.pallas import tpu as pltpu

D_MODEL = 1024
DEPTH = 4
N_A_LAYERS = 2
TOK_WIDTH = 512
MEM_WIDTH = 512
A_CHUNK = 128
A_GROUPS = 8
B_HEADS = 8
B_HEAD_DIM = 64
CHUNK = 64
B_LEFT_CHUNKS = 8
B_WINDOW = B_LEFT_CHUNKS * CHUNK
REL_CLIP = 256
N_MEM = 256
MEM_HEADS = 4
MEM_HEAD_DIM = 128
D_FF = 2816
CONV_W = 3
EPS = 1e-6
NEG_INF = -1e30

LANES = 128
SUBLANES = 8
PROMPT_TILE = 512
FF_CHUNK = 256
N_FF_CHUNKS = D_FF // FF_CHUNK
FF_SLOTS = 3
FF_SPLIT = 6 * FF_CHUNK
Q_BLOCK = 256
VMEM_LIMIT = 56 * 1024 * 1024

F32 = jnp.float32
BF16 = jnp.bfloat16


def _rms(x, g):
    ms = jnp.mean(x * x, axis=-1, keepdims=True)
    return x * lax.rsqrt(ms + EPS) * g


def _softmax_parts(s):
    m = jnp.max(s, axis=-1, keepdims=True)
    p = jnp.exp(s - m)
    return p, jnp.sum(p, axis=-1, keepdims=True)


def _mem_attend(q_mem, mkT_ref, mv_ref):
    outs = []
    for hh in range(MEM_HEADS):
        sl = slice(hh * MEM_HEAD_DIM, (hh + 1) * MEM_HEAD_DIM)
        q = (q_mem[:, sl] * (MEM_HEAD_DIM ** -0.5)).astype(BF16)
        s = jnp.dot(q, mkT_ref[0, sl, :], preferred_element_type=F32)
        p, l = _softmax_parts(s)
        o = jnp.dot(p.astype(BF16), mv_ref[0, :, sl], preferred_element_type=F32)
        outs.append(o * (1.0 / l))
    return jnp.concatenate(outs, axis=-1)


def _spatial_gate(u, v, ws_ref, bs_ref, chunk):
    rows = u.shape[0]
    r = lax.broadcasted_iota(jnp.int32, (A_CHUNK, A_CHUNK), 0)
    c = lax.broadcasted_iota(jnp.int32, (A_CHUNK, A_CHUNK), 1)
    w_tril = [jnp.where(r >= c, ws_ref[g], 0.0).astype(BF16)[:chunk, :] for g in range(A_GROUPS)]
    lane = lax.broadcasted_iota(jnp.int32, (A_CHUNK, LANES), 1)
    low = lane < (LANES // 2)
    bias = bs_ref[:chunk, :]
    toks = []
    for ci in range(rows // chunk):
        vc = v[ci * chunk:(ci + 1) * chunk]
        if chunk < A_CHUNK:
            vc = jnp.concatenate([vc, jnp.zeros((A_CHUNK - chunk, TOK_WIDTH), F32)], axis=0)
        parts = []
        for pr in range(A_GROUPS // 2):
            vp = vc[:, pr * LANES:(pr + 1) * LANES]
            v_lo = jnp.where(low, vp, 0.0).astype(BF16)
            v_hi = jnp.where(low, 0.0, vp).astype(BF16)
            parts.append(jnp.dot(w_tril[2 * pr], v_lo, preferred_element_type=F32)
                         + jnp.dot(w_tril[2 * pr + 1], v_hi, preferred_element_type=F32))
        s = jnp.concatenate(parts, axis=-1) + bias
        toks.append(u[ci * chunk:(ci + 1) * chunk] * s)
    return jnp.concatenate(toks, axis=0) if len(toks) > 1 else toks[0]


def _mixer_a_kernel(h_ref, gmix_ref, win_ref, gv_ref, ws_ref, bs_ref, mkT_ref, mv_ref, wout_ref,
                    *out_refs, chunk, emit_v):
    out_ref = out_refs[0]
    x = h_ref[0]
    xn = _rms(x, gmix_ref[...]).astype(BF16)
    z = jnp.dot(xn, win_ref[...], preferred_element_type=F32)
    uv = jax.nn.gelu(z[:, :2 * TOK_WIDTH])
    u = uv[:, :TOK_WIDTH]
    v = _rms(uv[:, TOK_WIDTH:], gv_ref[...])
    if emit_v:
        out_refs[1][0] = v
    tok = _spatial_gate(u, v, ws_ref, bs_ref, chunk)
    mo = _mem_attend(z[:, 2 * TOK_WIDTH:], mkT_ref, mv_ref)
    cat = jnp.concatenate([tok, mo], axis=-1).astype(BF16)
    out_ref[0] = x + jnp.dot(cat, wout_ref[...], preferred_element_type=F32)


def _band_attend(q, kT_win, v_win, bias_ref, blocks, first_tile):
    qb, kb = bias_ref.shape[1], bias_ref.shape[2]
    lane_q = lax.broadcasted_iota(jnp.int32, (qb, LANES), 1)
    low = lane_q < B_HEAD_DIM
    lane_k = lax.broadcasted_iota(jnp.int32, (1, kb), 1)
    qs = q * (B_HEAD_DIM ** -0.5)
    row_parts = []
    for (row0, key0, n_prev) in blocks:
        pm = None
        if n_prev:
            pm = jnp.where(lane_k < n_prev, jnp.where(first_tile, NEG_INF, 0.0), 0.0)
        pair_outs = []
        for pr in range(B_HEADS // 2):
            psl = slice(pr * LANES, (pr + 1) * LANES)
            qp = qs[row0:row0 + qb, psl]
            kT = kT_win[psl, key0:key0 + kb]
            vv = v_win[key0:key0 + kb, psl]
            halves = []
            for half in range(2):
                qh = (jnp.where(low, qp, 0.0) if half == 0 else jnp.where(low, 0.0, qp)).astype(BF16)
                s = jnp.dot(qh, kT, preferred_element_type=F32) + bias_ref[2 * pr + half]
                if pm is not None:
                    s = s + pm
                p, l = _softmax_parts(s)
                o = jnp.dot(p.astype(BF16), vv, preferred_element_type=F32)
                halves.append(o * (1.0 / l))
            pair_outs.append(jnp.where(low, halves[0], halves[1]))
        row_parts.append(jnp.concatenate(pair_outs, axis=-1))
    return jnp.concatenate(row_parts, axis=0) if len(row_parts) > 1 else row_parts[0]


def _mixer_b_kernel(h_ref, gmix_ref, win_ref, kTa_ref, kTb_ref, va_ref, vb_ref, bias_ref,
                    mkT_ref, mv_ref, wout_ref, out_ref, *, blocks):
    x = h_ref[0]
    xn = _rms(x, gmix_ref[...]).astype(BF16)
    z = jnp.dot(xn, win_ref[...], preferred_element_type=F32)
    kT_win = jnp.concatenate([kTa_ref[0], kTb_ref[0]], axis=1)
    v_win = jnp.concatenate([va_ref[0], vb_ref[0]], axis=0)
    tok = _band_attend(z[:, :TOK_WIDTH], kT_win, v_win, bias_ref, blocks, pl.program_id(1) == 0)
    mo = _mem_attend(z[:, TOK_WIDTH:], mkT_ref, mv_ref)
    cat = jnp.concatenate([tok, mo], axis=-1).astype(BF16)
    out_ref[0] = x + jnp.dot(cat, wout_ref[...], preferred_element_type=F32)


def _ffn_kernel(h_ref, g_ref, wup_ref, cw_ref, cb_ref, wd_ref, prev_ref, gfin_ref,
                out_ref, state_ref, carry_ref, xn_ref, a_ref, gate_ref, act_ref, *, final_norm):
    @pl.when(pl.program_id(1) == 0)
    def _():
        carry_ref[...] = prev_ref[...]

    n_seq, seq_len, _ = h_ref.shape
    rows = n_seq * seq_len
    x = h_ref[...].reshape(rows, D_MODEL)
    xn_ref[...] = _rms(x, g_ref[...]).astype(BF16)
    head_row = lax.broadcasted_iota(jnp.int32, (SUBLANES, FF_CHUNK), 0)

    for f in range(N_FF_CHUNKS):
        slot = f % FF_SLOTS
        fsl = slice(f * FF_CHUNK, (f + 1) * FF_CHUNK)
        a_ref[slot] = jnp.dot(xn_ref[...], wup_ref[:, fsl], preferred_element_type=F32)
        gate_ref[slot] = jnp.dot(xn_ref[...], wup_ref[:, D_FF + f * FF_CHUNK:D_FF + (f + 1) * FF_CHUNK],
                                 preferred_element_type=F32)
        a = a_ref[slot]
        gate = gate_ref[slot]
        w0, w1, w2 = cw_ref[0:1, fsl], cw_ref[1:2, fsl], cw_ref[2:3, fsl]
        ys = []
        for s in range(n_seq):
            a_s = a[s * seq_len:(s + 1) * seq_len]
            p0, p1 = carry_ref[s, 0:1, fsl], carry_ref[s, 1:2, fsl]
            r1, r2 = pltpu.roll(a_s, 1, 0), pltpu.roll(a_s, 2, 0)
            a1 = jnp.concatenate([jnp.where(head_row == 0, p1, r1[:SUBLANES]), r1[SUBLANES:]], axis=0)
            a2 = jnp.concatenate([jnp.where(head_row == 0, p0, jnp.where(head_row == 1, p1, r2[:SUBLANES])),
                                  r2[SUBLANES:]], axis=0)
            carry_ref[s, :, fsl] = a_s[seq_len - 2:seq_len]
            ys.append(cb_ref[:, fsl] + w0 * a2 + w1 * a1 + w2 * a_s)
        y = jnp.concatenate(ys, axis=0) if n_seq > 1 else ys[0]
        act_ref[:, fsl] = (jax.nn.gelu(y) * gate).astype(BF16)

    state_ref[...] = carry_ref[...]
    hn = x
    for lo, hi in ((0, FF_SPLIT), (FF_SPLIT, D_FF)):
        hn = hn + jnp.dot(act_ref[:, lo:hi], wd_ref[lo:hi, :], preferred_element_type=F32)
    out = _rms(hn, gfin_ref[...]) if final_norm else hn
    out_ref[...] = out.reshape(n_seq, seq_len, D_MODEL)


def _kv_kernel(h_ref, g_ref, w_ref, kT_ref, v_ref, kf_ref, vf_ref):
    hn = _rms(h_ref[0], g_ref[...]).astype(BF16)
    kv = jnp.dot(hn, w_ref[...], preferred_element_type=F32)
    k, v = kv[:, :TOK_WIDTH], kv[:, TOK_WIDTH:]
    kf_ref[0] = k
    vf_ref[0] = v
    kT_ref[0] = k.T.astype(BF16)
    v_ref[0] = v.astype(BF16)


def _mem_kv_kernel(m_ref, g_ref, w_ref, k_ref, v_ref, kT_ref, vb_ref):
    mn = _rms(m_ref[0], g_ref[0]).astype(BF16)
    kv = jnp.dot(mn, w_ref[0], preferred_element_type=F32)
    k, v = kv[:, :MEM_WIDTH], kv[:, MEM_WIDTH:]
    k_ref[0, 0] = k
    v_ref[0, 0] = v
    kT_ref[0, 0] = k.T.astype(BF16)
    vb_ref[0, 0] = v.astype(BF16)


def _const_spec(shape):
    nd = len(shape)
    return pl.BlockSpec(shape, lambda b, i: (0,) * nd, pipeline_mode=pl.Buffered(1))


def _params():
    return pltpu.CompilerParams(dimension_semantics=("arbitrary", "arbitrary"),
                                vmem_limit_bytes=VMEM_LIMIT)


def _mixer_a(h, tile, gmix, win, gv, ws, bs_full, mkT, mv, wout, emit_v):
    nb, seq, _ = h.shape
    chunk = min(A_CHUNK, tile)
    act_spec = pl.BlockSpec((1, tile, D_MODEL), lambda b, i: (b, i, 0))
    out_shape = [jax.ShapeDtypeStruct(h.shape, F32)]
    out_specs = [act_spec]
    if emit_v:
        out_shape.append(jax.ShapeDtypeStruct((nb, seq, TOK_WIDTH), F32))
        out_specs.append(pl.BlockSpec((1, tile, TOK_WIDTH), lambda b, i: (b, i, 0)))
    res = pl.pallas_call(
        functools.partial(_mixer_a_kernel, chunk=chunk, emit_v=emit_v),
        grid=(nb, seq // tile),
        in_specs=[act_spec, _const_spec(gmix.shape), _const_spec(win.shape), _const_spec(gv.shape),
                  _const_spec(ws.shape), _const_spec(bs_full.shape),
                  pl.BlockSpec((1, MEM_WIDTH, N_MEM), lambda b, i: (b, 0, 0)),
                  pl.BlockSpec((1, N_MEM, MEM_WIDTH), lambda b, i: (b, 0, 0)),
                  _const_spec(wout.shape)],
        out_specs=out_specs, out_shape=out_shape, compiler_params=_params(),
        name="mixer_a",
    )(h, gmix, win, gv, ws, bs_full, mkT, mv, wout)
    return res if emit_v else (res[0], None)


def _mixer_b(h, tile, gmix, win, kT_a, kT_b, v_a, v_b, bias, blocks, mkT, mv, wout, prompt):
    nb, seq, _ = h.shape
    act_spec = pl.BlockSpec((1, tile, D_MODEL), lambda b, i: (b, i, 0))
    if prompt:
        ka = pl.BlockSpec((1, TOK_WIDTH, tile), lambda b, i: (b, 0, jnp.maximum(i - 1, 0)))
        kb = pl.BlockSpec((1, TOK_WIDTH, tile), lambda b, i: (b, 0, i))
        va = pl.BlockSpec((1, tile, TOK_WIDTH), lambda b, i: (b, jnp.maximum(i - 1, 0), 0))
        vb = pl.BlockSpec((1, tile, TOK_WIDTH), lambda b, i: (b, i, 0))
    else:
        ka = pl.BlockSpec((1,) + kT_a.shape[1:], lambda b, i: (b, 0, 0))
        kb = pl.BlockSpec((1,) + kT_b.shape[1:], lambda b, i: (b, 0, 0))
        va = pl.BlockSpec((1,) + v_a.shape[1:], lambda b, i: (b, 0, 0))
        vb = pl.BlockSpec((1,) + v_b.shape[1:], lambda b, i: (b, 0, 0))
    return pl.pallas_call(
        functools.partial(_mixer_b_kernel, blocks=blocks),
        grid=(nb, seq // tile),
        in_specs=[act_spec, _const_spec(gmix.shape), _const_spec(win.shape), ka, kb, va, vb,
                  _const_spec(bias.shape),
                  pl.BlockSpec((1, MEM_WIDTH, N_MEM), lambda b, i: (b, 0, 0)),
                  pl.BlockSpec((1, N_MEM, MEM_WIDTH), lambda b, i: (b, 0, 0)),
                  _const_spec(wout.shape)],
        out_specs=act_spec, out_shape=jax.ShapeDtypeStruct(h.shape, F32),
        compiler_params=_params(), name="mixer_b",
    )(h, gmix, win, kT_a, kT_b, v_a, v_b, bias, mkT, mv, wout)


def _ffn(h, n_seq, tile, g, wup, cw, cb, wd, prev, gfin, final_norm):
    nb, seq, _ = h.shape
    act_spec = pl.BlockSpec((n_seq, tile, D_MODEL), lambda b, i: (b, i, 0))
    st_spec = pl.BlockSpec((n_seq, CONV_W - 1, D_FF), lambda b, i: (b, 0, 0))
    return pl.pallas_call(
        functools.partial(_ffn_kernel, final_norm=final_norm),
        grid=(nb // n_seq, seq // tile),
        in_specs=[act_spec, _const_spec(g.shape), _const_spec(wup.shape), _const_spec(cw.shape),
                  _const_spec(cb.shape), _const_spec(wd.shape), st_spec, _const_spec(gfin.shape)],
        out_specs=[act_spec, st_spec],
        out_shape=[jax.ShapeDtypeStruct(h.shape, F32),
                   jax.ShapeDtypeStruct((nb, CONV_W - 1, D_FF), F32)],
        scratch_shapes=[pltpu.VMEM((n_seq, CONV_W - 1, D_FF), F32),
                        pltpu.VMEM((n_seq * tile, D_MODEL), BF16),
                        pltpu.VMEM((FF_SLOTS, n_seq * tile, FF_CHUNK), F32),
                        pltpu.VMEM((FF_SLOTS, n_seq * tile, FF_CHUNK), F32),
                        pltpu.VMEM((n_seq * tile, D_FF), BF16)],
        compiler_params=_params(), name="conv_ffn",
    )(h, g, wup, cw, cb, wd, prev, gfin)


def _shared_kv(h, tile, g, w):
    nb, seq, _ = h.shape
    n_t = seq // tile
    act_spec = pl.BlockSpec((1, tile, D_MODEL), lambda b, i: (b, i, 0))
    last_spec = pl.BlockSpec((1, tile, TOK_WIDTH), lambda b, i: (b, 0, 0))
    return pl.pallas_call(
        _kv_kernel,
        grid=(nb, n_t),
        in_specs=[act_spec, _const_spec(g.shape), _const_spec(w.shape)],
        out_specs=[pl.BlockSpec((1, TOK_WIDTH, tile), lambda b, i: (b, 0, i)),
                   pl.BlockSpec((1, tile, TOK_WIDTH), lambda b, i: (b, i, 0)),
                   last_spec, last_spec],
        out_shape=[jax.ShapeDtypeStruct((nb, TOK_WIDTH, seq), BF16),
                   jax.ShapeDtypeStruct((nb, seq, TOK_WIDTH), BF16),
                   jax.ShapeDtypeStruct((nb, tile, TOK_WIDTH), F32),
                   jax.ShapeDtypeStruct((nb, tile, TOK_WIDTH), F32)],
        compiler_params=_params(), name="shared_kv",
    )(h, g, w)


def _mem_kv(mem, g, w):
    nb = mem.shape[0]
    f32_spec = pl.BlockSpec((1, 1, N_MEM, MEM_WIDTH), lambda l, b: (l, b, 0, 0))
    return pl.pallas_call(
        _mem_kv_kernel,
        grid=(DEPTH, nb),
        in_specs=[pl.BlockSpec((1, N_MEM, D_MODEL), lambda l, b: (b, 0, 0)),
                  pl.BlockSpec((1, 1, D_MODEL), lambda l, b: (l, 0, 0)),
                  pl.BlockSpec((1, D_MODEL, 2 * MEM_WIDTH), lambda l, b: (l, 0, 0))],
        out_specs=[f32_spec, f32_spec,
                   pl.BlockSpec((1, 1, MEM_WIDTH, N_MEM), lambda l, b: (l, b, 0, 0)),
                   f32_spec],
        out_shape=[jax.ShapeDtypeStruct((DEPTH, nb, N_MEM, MEM_WIDTH), F32),
                   jax.ShapeDtypeStruct((DEPTH, nb, N_MEM, MEM_WIDTH), F32),
                   jax.ShapeDtypeStruct((DEPTH, nb, MEM_WIDTH, N_MEM), BF16),
                   jax.ShapeDtypeStruct((DEPTH, nb, N_MEM, MEM_WIDTH), BF16)],
        compiler_params=_params(), name="mem_kv",
    )(mem, g, w)


def _band_bias(rel_bias, q_chunks, k_chunks):
    nq, nk = q_chunks * CHUNK, k_chunks * CHUNK
    period = nq + nk
    d = np.arange(period)
    d = np.where(d < nk, d, d - period)
    idx = np.clip(B_WINDOW - d, -REL_CLIP, REL_CLIP) + REL_CLIP
    e = rel_bias.astype(F32)[..., idx]
    lead = e.shape[:-1]
    flat = jnp.tile(e, (1,) * len(lead) + (nq,))[..., :nq * (period - 1)]
    toep = flat.reshape(lead + (nq, period - 1))[..., :nk]
    dc = np.arange(nk)[None, :] // CHUNK - np.arange(nq)[:, None] // CHUNK
    valid = (dc >= 0) & (dc <= B_LEFT_CHUNKS)
    return jnp.where(valid, toep, NEG_INF)


def _trunk(h, tile, prompt, mkT, mv, conv_prev, cache_kT, cache_v, wts):
    a_v, states = [], []
    k_last = v_last = None
    ffn_seqs = max(1, PROMPT_TILE // tile)
    for l in range(DEPTH):
        if l < N_A_LAYERS:
            h, v_rows = _mixer_a(h, tile, wts["g_mix"][l], wts["w_in_a"][l], wts["g_v"][l], wts["w_s"][l],
                                 wts["bs_full"][l], mkT[l], mv[l], wts["w_out"][l], emit_v=not prompt)
            if not prompt:
                a_v.append(v_rows)
        else:
            j = l - N_A_LAYERS
            if j == 0:
                kT, v_bf, k_last, v_last = _shared_kv(h, tile, wts["g_kv"], wts["w_kv"])
            if prompt:
                blocks = tuple((qb * Q_BLOCK, qb * Q_BLOCK, B_WINDOW - qb * Q_BLOCK)
                               for qb in range(tile // Q_BLOCK))
                h = _mixer_b(h, tile, wts["g_mix"][l], wts["w_in_b"][j], kT, kT, v_bf, v_bf,
                             wts["bias_prompt"][j], blocks, mkT[l], mv[l], wts["w_out"][l], True)
            else:
                h = _mixer_b(h, tile, wts["g_mix"][l], wts["w_in_b"][j], cache_kT, kT, cache_v, v_bf,
                             wts["bias_sample"][j], ((0, 0, 0),), mkT[l], mv[l], wts["w_out"][l], False)
        h, st = _ffn(h, ffn_seqs, tile, wts["g_ffn"][l], wts["w_up"][l], wts["conv_w"][l],
                     wts["conv_b"][l], wts["w_down"][l], conv_prev[l], wts["g_final"],
                     final_norm=(l == DEPTH - 1))
        states.append(st)
    return h, a_v, k_last, v_last, jnp.stack(states)


def kernel(x_prompt, x_sample, mem_prompt, cache_mem_k, cache_mem_v, cache_b_k, cache_b_v, state_conv,
           g_mix, w_in_a, g_v, w_s, b_s, g_kv, w_kv, w_in_b, rel_bias, g_mem, w_mem_kv,
           w_out, g_ffn, w_up, conv_w, conv_b, w_down, g_final):
    bsz = x_prompt.shape[0]
    dbs = x_sample.shape[0]
    dec_seq = x_sample.shape[1]
    wts = {
        "g_mix": g_mix.reshape(DEPTH, 1, D_MODEL),
        "w_in_a": w_in_a.astype(BF16),
        "g_v": g_v.reshape(N_A_LAYERS, 1, TOK_WIDTH),
        "w_s": w_s,
        "bs_full": jnp.repeat(b_s.transpose(0, 2, 1), TOK_WIDTH // A_GROUPS, axis=-1),
        "g_kv": g_kv.reshape(1, D_MODEL),
        "w_kv": w_kv.astype(BF16),
        "w_in_b": w_in_b.astype(BF16),
        "bias_prompt": _band_bias(rel_bias, Q_BLOCK // CHUNK, Q_BLOCK // CHUNK + B_LEFT_CHUNKS),
        "bias_sample": _band_bias(rel_bias, 1, 1 + B_LEFT_CHUNKS),
        "w_out": w_out.astype(BF16),
        "g_ffn": g_ffn.reshape(DEPTH, 1, D_MODEL),
        "w_up": w_up.astype(BF16),
        "conv_w": conv_w,
        "conv_b": conv_b.reshape(DEPTH, 1, D_FF),
        "w_down": w_down.astype(BF16),
        "g_final": g_final.reshape(1, D_MODEL),
    }

    mem_k_p, mem_v_p, mkT_p, mv_p = _mem_kv(mem_prompt, g_mem.reshape(DEPTH, 1, D_MODEL),
                                            w_mem_kv.astype(BF16))
    conv0 = jnp.zeros((DEPTH, bsz, CONV_W - 1, D_FF), F32)
    y_p, _, k_p, v_p, conv_p = _trunk(x_prompt, PROMPT_TILE, True, mkT_p, mv_p, conv0, None, None, wts)

    mkT_s = cache_mem_k.reshape(DEPTH, dbs, N_MEM, MEM_WIDTH).transpose(0, 1, 3, 2).astype(BF16)
    mv_s = cache_mem_v.reshape(DEPTH, dbs, N_MEM, MEM_WIDTH).astype(BF16)
    cache_kT = cache_b_k.reshape(dbs, -1, TOK_WIDTH).transpose(0, 2, 1).astype(BF16)
    cache_v = cache_b_v.reshape(dbs, -1, TOK_WIDTH).astype(BF16)
    y_s, a_v_s, k_s, v_s, conv_s = _trunk(x_sample, dec_seq, False, mkT_s, mv_s, state_conv,
                                          cache_kT, cache_v, wts)

    hd = (B_HEADS, B_HEAD_DIM)
    mh = (MEM_HEADS, MEM_HEAD_DIM)
    return (y_p, y_s,
            mem_k_p.reshape(DEPTH, bsz, N_MEM, *mh), mem_v_p.reshape(DEPTH, bsz, N_MEM, *mh),
            k_p.reshape(bsz, B_WINDOW, *hd), v_p.reshape(bsz, B_WINDOW, *hd),
            conv_p,
            jnp.stack(a_v_s),
            k_s.reshape(dbs, dec_seq, *hd), v_s.reshape(dbs, dec_seq, *hd),
            conv_s)
```

```python
import functools

import numpy as np
import jax
import jax.numpy as jnp
from jax import lax
from jax.experimental import pallas as pl
from jax.experimental.pallas import tpu as pltpu

D_MODEL = 1024
DEPTH = 4
N_A_LAYERS = 2
TOK_WIDTH = 512
MEM_WIDTH = 512
A_CHUNK = 128
A_GROUPS = 8
B_HEADS = 8
B_HEAD_DIM = 64
CHUNK = 64
B_LEFT_CHUNKS = 8
B_WINDOW = B_LEFT_CHUNKS * CHUNK
REL_CLIP = 256
N_MEM = 256
MEM_HEADS = 4
MEM_HEAD_DIM = 128
D_FF = 2816
CONV_W = 3
EPS = 1e-6
NEG_INF = -1e30

LANES = 128
SUBLANES = 8
PROMPT_TILE = 512
FF_CHUNK = 256
N_FF_CHUNKS = D_FF // FF_CHUNK
FF_SLOTS = 3
ATT_AHEAD = 3
ATT_SLOTS = ATT_AHEAD + 1
FF_SPLIT = 6 * FF_CHUNK
Q_BLOCK = 256
VMEM_LIMIT = 56 * 1024 * 1024

F32 = jnp.float32
BF16 = jnp.bfloat16


def _rms(x, g):
    ms = jnp.mean(x * x, axis=-1, keepdims=True)
    return x * lax.rsqrt(ms + EPS) * g


def _softmax_parts(s):
    m = jnp.max(s, axis=-1, keepdims=True)
    p = jnp.exp(s - m)
    return p, jnp.sum(p, axis=-1, keepdims=True)


def _run_skewed(stages, ahead):
    n = len(stages)
    for t in range(n + ahead):
        if t < n:
            stages[t][0]()
        if t >= ahead:
            stages[t - ahead][1]()


def _mem_stages(z_ref, col0, mkT_ref, mv_ref, s_ref, cat_ref, first_index):
    stages = []
    for hh in range(MEM_HEADS):
        sl = slice(hh * MEM_HEAD_DIM, (hh + 1) * MEM_HEAD_DIM)
        slot = (first_index + hh) % s_ref.shape[0]

        def score(sl=sl, slot=slot):
            q = (z_ref[:, col0 + sl.start:col0 + sl.stop] * (MEM_HEAD_DIM ** -0.5)).astype(BF16)
            s_ref[slot] = jnp.dot(q, mkT_ref[0, sl, :], preferred_element_type=F32)

        def finish(sl=sl, slot=slot):
            p, l = _softmax_parts(s_ref[slot])
            o = jnp.dot(p.astype(BF16), mv_ref[0, :, sl], preferred_element_type=F32)
            cat_ref[:, TOK_WIDTH + sl.start:TOK_WIDTH + sl.stop] = (o * (1.0 / l)).astype(BF16)

        stages.append((score, finish))
    return stages


def _spatial_gate(u, v, ws_ref, bs_ref, chunk):
    rows = u.shape[0]
    r = lax.broadcasted_iota(jnp.int32, (A_CHUNK, A_CHUNK), 0)
    c = lax.broadcasted_iota(jnp.int32, (A_CHUNK, A_CHUNK), 1)
    w_tril = [jnp.where(r >= c, ws_ref[g], 0.0).astype(BF16)[:chunk, :] for g in range(A_GROUPS)]
    lane = lax.broadcasted_iota(jnp.int32, (A_CHUNK, LANES), 1)
    low = lane < (LANES // 2)
    bias = bs_ref[:chunk, :]
    toks = []
    for ci in range(rows // chunk):
        vc = v[ci * chunk:(ci + 1) * chunk]
        if chunk < A_CHUNK:
            vc = jnp.concatenate([vc, jnp.zeros((A_CHUNK - chunk, TOK_WIDTH), F32)], axis=0)
        parts = []
        for pr in range(A_GROUPS // 2):
            vp = vc[:, pr * LANES:(pr + 1) * LANES]
            v_lo = jnp.where(low, vp, 0.0).astype(BF16)
            v_hi = jnp.where(low, 0.0, vp).astype(BF16)
            parts.append(jnp.dot(w_tril[2 * pr], v_lo, preferred_element_type=F32)
                         + jnp.dot(w_tril[2 * pr + 1], v_hi, preferred_element_type=F32))
        s = jnp.concatenate(parts, axis=-1) + bias
        toks.append(u[ci * chunk:(ci + 1) * chunk] * s)
    return jnp.concatenate(toks, axis=0) if len(toks) > 1 else toks[0]


def _mixer_a_kernel(h_ref, gmix_ref, win_ref, gv_ref, ws_ref, bs_ref, mkT_ref, mv_ref, wout_ref,
                    *refs, chunk, emit_v):
    out_ref = refs[0]
    z_ref, ms_ref, cat_ref = refs[-3:]
    x = h_ref[0]
    xn = _rms(x, gmix_ref[...]).astype(BF16)
    z_ref[...] = jnp.dot(xn, win_ref[...], preferred_element_type=F32)
    stages = _mem_stages(z_ref, 2 * TOK_WIDTH, mkT_ref, mv_ref, ms_ref, cat_ref, 0)
    for score, _ in stages:
        score()
    uv = jax.nn.gelu(z_ref[:, :2 * TOK_WIDTH])
    u = uv[:, :TOK_WIDTH]
    v = _rms(uv[:, TOK_WIDTH:], gv_ref[...])
    if emit_v:
        refs[1][0] = v
    cat_ref[:, :TOK_WIDTH] = _spatial_gate(u, v, ws_ref, bs_ref, chunk).astype(BF16)
    for _, finish in stages:
        finish()
    out_ref[0] = x + jnp.dot(cat_ref[...], wout_ref[...], preferred_element_type=F32)


def _band_stages(z_ref, kT_win, v_win, bias_ref, blocks, first_tile, s_ref, cat_ref):
    qb, kb = bias_ref.shape[2], bias_ref.shape[3]
    lane_q = lax.broadcasted_iota(jnp.int32, (qb, LANES), 1)
    low = lane_q < B_HEAD_DIM
    stages = []
    held = {}
    for (row0, key0, first_variant) in blocks:
        variant = jnp.where(first_tile, first_variant, 0) if first_variant else 0
        for pr in range(B_HEADS // 2):
            psl = slice(pr * LANES, (pr + 1) * LANES)
            for half in range(2):
                slot = len(stages) % s_ref.shape[0]

                def score(row0=row0, key0=key0, variant=variant, psl=psl, pr=pr, half=half, slot=slot):
                    qp = z_ref[row0:row0 + qb, psl] * (B_HEAD_DIM ** -0.5)
                    qh = (jnp.where(low, qp, 0.0) if half == 0 else jnp.where(low, 0.0, qp)).astype(BF16)
                    s = jnp.dot(qh, kT_win[psl, key0:key0 + kb], preferred_element_type=F32)
                    s_ref[slot] = s + bias_ref[variant, 2 * pr + half]

                def finish(row0=row0, key0=key0, psl=psl, half=half, slot=slot):
                    p, l = _softmax_parts(s_ref[slot])
                    o = jnp.dot(p.astype(BF16), v_win[key0:key0 + kb, psl], preferred_element_type=F32)
                    o = o * (1.0 / l)
                    if half == 0:
                        held[(row0, psl.start)] = o
                    else:
                        pair = jnp.where(low, held.pop((row0, psl.start)), o)
                        cat_ref[row0:row0 + qb, psl] = pair.astype(BF16)

                stages.append((score, finish))
    return stages


def _mixer_b_kernel(h_ref, gmix_ref, win_ref, kTa_ref, kTb_ref, va_ref, vb_ref, bias_ref,
                    mkT_ref, mv_ref, wout_ref, out_ref, z_ref, s_ref, ms_ref, cat_ref, *, blocks):
    x = h_ref[0]
    xn = _rms(x, gmix_ref[...]).astype(BF16)
    z_ref[...] = jnp.dot(xn, win_ref[...], preferred_element_type=F32)
    kT_win = jnp.concatenate([kTa_ref[0], kTb_ref[0]], axis=1)
    v_win = jnp.concatenate([va_ref[0], vb_ref[0]], axis=0)
    stages = _band_stages(z_ref, kT_win, v_win, bias_ref, blocks, pl.program_id(1) == 0, s_ref, cat_ref)
    stages += _mem_stages(z_ref, TOK_WIDTH, mkT_ref, mv_ref, ms_ref, cat_ref, len(stages))
    _run_skewed(stages, ATT_AHEAD)
    out_ref[0] = x + jnp.dot(cat_ref[...], wout_ref[...], preferred_element_type=F32)


def _ffn_kernel(h_ref, g_ref, wup_ref, cw_ref, cb_ref, wd_ref, prev_ref, gfin_ref,
                out_ref, state_ref, carry_ref, xn_ref, a_ref, gate_ref, act_ref, *, final_norm):
    @pl.when(pl.program_id(1) == 0)
    def _():
        carry_ref[...] = prev_ref[...]

    n_seq, seq_len, _ = h_ref.shape
    rows = n_seq * seq_len
    x = h_ref[...].reshape(rows, D_MODEL)
    xn_ref[...] = _rms(x, g_ref[...]).astype(BF16)
    head_row = lax.broadcasted_iota(jnp.int32, (SUBLANES, FF_CHUNK), 0)

    for f in range(N_FF_CHUNKS):
        slot = f % FF_SLOTS
        fsl = slice(f * FF_CHUNK, (f + 1) * FF_CHUNK)
        a_ref[slot] = jnp.dot(xn_ref[...], wup_ref[:, fsl], preferred_element_type=F32)
        gate_ref[slot] = jnp.dot(xn_ref[...], wup_ref[:, D_FF + f * FF_CHUNK:D_FF + (f + 1) * FF_CHUNK],
                                 preferred_element_type=F32)
        a = a_ref[slot]
        gate = gate_ref[slot]
        w0, w1, w2 = cw_ref[0:1, fsl], cw_ref[1:2, fsl], cw_ref[2:3, fsl]
        ys = []
        for s in range(n_seq):
            a_s = a[s * seq_len:(s + 1) * seq_len]
            p0, p1 = carry_ref[s, 0:1, fsl], carry_ref[s, 1:2, fsl]
            r1, r2 = pltpu.roll(a_s, 1, 0), pltpu.roll(a_s, 2, 0)
            a1 = jnp.concatenate([jnp.where(head_row == 0, p1, r1[:SUBLANES]), r1[SUBLANES:]], axis=0)
            a2 = jnp.concatenate([jnp.where(head_row == 0, p0, jnp.where(head_row == 1, p1, r2[:SUBLANES])),
                                  r2[SUBLANES:]], axis=0)
            carry_ref[s, :, fsl] = a_s[seq_len - 2:seq_len]
            ys.append(cb_ref[:, fsl] + w0 * a2 + w1 * a1 + w2 * a_s)
        y = jnp.concatenate(ys, axis=0) if n_seq > 1 else ys[0]
        act_ref[:, fsl] = (jax.nn.gelu(y) * gate).astype(BF16)

    state_ref[...] = carry_ref[...]
    hn = x
    for lo, hi in ((0, FF_SPLIT), (FF_SPLIT, D_FF)):
        hn = hn + jnp.dot(act_ref[:, lo:hi], wd_ref[lo:hi, :], preferred_element_type=F32)
    out = _rms(hn, gfin_ref[...]) if final_norm else hn
    out_ref[...] = out.reshape(n_seq, seq_len, D_MODEL)


def _kv_kernel(h_ref, g_ref, w_ref, kT_ref, v_ref, kf_ref, vf_ref):
    hn = _rms(h_ref[0], g_ref[...]).astype(BF16)
    kv = jnp.dot(hn, w_ref[...], preferred_element_type=F32)
    k, v = kv[:, :TOK_WIDTH], kv[:, TOK_WIDTH:]
    kf_ref[0] = k
    vf_ref[0] = v
    kT_ref[0] = k.T.astype(BF16)
    v_ref[0] = v.astype(BF16)


def _mem_kv_kernel(m_ref, g_ref, w_ref, k_ref, v_ref, kT_ref, vb_ref):
    mn = _rms(m_ref[0], g_ref[0]).astype(BF16)
    kv = jnp.dot(mn, w_ref[0], preferred_element_type=F32)
    k, v = kv[:, :MEM_WIDTH], kv[:, MEM_WIDTH:]
    k_ref[0, 0] = k
    v_ref[0, 0] = v
    kT_ref[0, 0] = k.T.astype(BF16)
    vb_ref[0, 0] = v.astype(BF16)


def _const_spec(shape):
    nd = len(shape)
    return pl.BlockSpec(shape, lambda b, i: (0,) * nd, pipeline_mode=pl.Buffered(1))


def _layer_spec(stacked, l):
    nd = stacked.ndim
    return pl.BlockSpec((None,) + stacked.shape[1:], lambda b, i: (l,) + (0,) * (nd - 1),
                        pipeline_mode=pl.Buffered(1))


def _mem_specs(l):
    return [pl.BlockSpec((None, 1, MEM_WIDTH, N_MEM), lambda b, i: (l, b, 0, 0)),
            pl.BlockSpec((None, 1, N_MEM, MEM_WIDTH), lambda b, i: (l, b, 0, 0))]


def _params():
    return pltpu.CompilerParams(dimension_semantics=("arbitrary", "arbitrary"),
                                vmem_limit_bytes=VMEM_LIMIT)


def _mixer_a(h, tile, l, wts, mkT, mv, emit_v):
    nb, seq, _ = h.shape
    chunk = min(A_CHUNK, tile)
    act_spec = pl.BlockSpec((1, tile, D_MODEL), lambda b, i: (b, i, 0))
    out_shape = [jax.ShapeDtypeStruct(h.shape, F32)]
    out_specs = [act_spec]
    if emit_v:
        out_shape.append(jax.ShapeDtypeStruct((nb, seq, TOK_WIDTH), F32))
        out_specs.append(pl.BlockSpec((1, tile, TOK_WIDTH), lambda b, i: (b, i, 0)))
    stacked = [wts[k] for k in ("g_mix", "w_in_a", "g_v", "w_s", "bs_full")]
    res = pl.pallas_call(
        functools.partial(_mixer_a_kernel, chunk=chunk, emit_v=emit_v),
        grid=(nb, seq // tile),
        in_specs=[act_spec] + [_layer_spec(w, l) for w in stacked] + _mem_specs(l)
        + [_layer_spec(wts["w_out"], l)],
        out_specs=out_specs, out_shape=out_shape,
        scratch_shapes=[pltpu.VMEM((tile, 2 * TOK_WIDTH + MEM_WIDTH), F32),
                        pltpu.VMEM((MEM_HEADS, tile, N_MEM), F32),
                        pltpu.VMEM((tile, D_MODEL), BF16)],
        compiler_params=_params(), name="mixer_a",
    )(h, *stacked, mkT, mv, wts["w_out"])
    return res if emit_v else (res[0], None)


def _mixer_b(h, tile, l, wts, kT_a, kT_b, v_a, v_b, bias, blocks, mkT, mv, prompt):
    nb, seq, _ = h.shape
    j = l - N_A_LAYERS
    act_spec = pl.BlockSpec((1, tile, D_MODEL), lambda b, i: (b, i, 0))
    if prompt:
        ka = pl.BlockSpec((1, TOK_WIDTH, tile), lambda b, i: (b, 0, jnp.maximum(i - 1, 0)))
        kb = pl.BlockSpec((1, TOK_WIDTH, tile), lambda b, i: (b, 0, i))
        va = pl.BlockSpec((1, tile, TOK_WIDTH), lambda b, i: (b, jnp.maximum(i - 1, 0), 0))
        vb = pl.BlockSpec((1, tile, TOK_WIDTH), lambda b, i: (b, i, 0))
    else:
        ka = pl.BlockSpec((1,) + kT_a.shape[1:], lambda b, i: (b, 0, 0))
        kb = pl.BlockSpec((1,) + kT_b.shape[1:], lambda b, i: (b, 0, 0))
        va = pl.BlockSpec((1,) + v_a.shape[1:], lambda b, i: (b, 0, 0))
        vb = pl.BlockSpec((1,) + v_b.shape[1:], lambda b, i: (b, 0, 0))
    return pl.pallas_call(
        functools.partial(_mixer_b_kernel, blocks=blocks),
        grid=(nb, seq // tile),
        in_specs=[act_spec, _layer_spec(wts["g_mix"], l), _layer_spec(wts["w_in_b"], j), ka, kb, va, vb,
                  _layer_spec(bias, j)] + _mem_specs(l) + [_layer_spec(wts["w_out"], l)],
        out_specs=act_spec, out_shape=jax.ShapeDtypeStruct(h.shape, F32),
        scratch_shapes=[pltpu.VMEM((tile, TOK_WIDTH + MEM_WIDTH), F32),
                        pltpu.VMEM((ATT_SLOTS,) + bias.shape[3:], F32),
                        pltpu.VMEM((ATT_SLOTS, tile, N_MEM), F32),
                        pltpu.VMEM((tile, D_MODEL), BF16)],
        compiler_params=_params(), name="mixer_b",
    )(h, wts["g_mix"], wts["w_in_b"], kT_a, kT_b, v_a, v_b, bias, mkT, mv, wts["w_out"])


def _ffn(h, n_seq, tile, l, wts, prev, final_norm):
    nb, seq, _ = h.shape
    act_spec = pl.BlockSpec((n_seq, tile, D_MODEL), lambda b, i: (b, i, 0))
    stacked = [wts[k] for k in ("g_ffn", "w_up", "conv_w", "conv_b", "w_down")]
    return pl.pallas_call(
        functools.partial(_ffn_kernel, final_norm=final_norm),
        grid=(nb // n_seq, seq // tile),
        in_specs=[act_spec] + [_layer_spec(w, l) for w in stacked]
        + [pl.BlockSpec((None, n_seq, CONV_W - 1, D_FF), lambda b, i: (l, b, 0, 0)),
           _const_spec(wts["g_final"].shape)],
        out_specs=[act_spec, pl.BlockSpec((n_seq, CONV_W - 1, D_FF), lambda b, i: (b, 0, 0))],
        out_shape=[jax.ShapeDtypeStruct(h.shape, F32),
                   jax.ShapeDtypeStruct((nb, CONV_W - 1, D_FF), F32)],
        scratch_shapes=[pltpu.VMEM((n_seq, CONV_W - 1, D_FF), F32),
                        pltpu.VMEM((n_seq * tile, D_MODEL), BF16),
                        pltpu.VMEM((FF_SLOTS, n_seq * tile, FF_CHUNK), F32),
                        pltpu.VMEM((FF_SLOTS, n_seq * tile, FF_CHUNK), F32),
                        pltpu.VMEM((n_seq * tile, D_FF), BF16)],
        compiler_params=_params(), name="conv_ffn",
    )(h, *stacked, prev, wts["g_final"])


def _shared_kv(h, tile, g, w):
    nb, seq, _ = h.shape
    n_t = seq // tile
    act_spec = pl.BlockSpec((1, tile, D_MODEL), lambda b, i: (b, i, 0))
    last_spec = pl.BlockSpec((1, tile, TOK_WIDTH), lambda b, i: (b, 0, 0))
    return pl.pallas_call(
        _kv_kernel,
        grid=(nb, n_t),
        in_specs=[act_spec, _const_spec(g.shape), _const_spec(w.shape)],
        out_specs=[pl.BlockSpec((1, TOK_WIDTH, tile), lambda b, i: (b, 0, i)),
                   pl.BlockSpec((1, tile, TOK_WIDTH), lambda b, i: (b, i, 0)),
                   last_spec, last_spec],
        out_shape=[jax.ShapeDtypeStruct((nb, TOK_WIDTH, seq), BF16),
                   jax.ShapeDtypeStruct((nb, seq, TOK_WIDTH), BF16),
                   jax.ShapeDtypeStruct((nb, tile, TOK_WIDTH), F32),
                   jax.ShapeDtypeStruct((nb, tile, TOK_WIDTH), F32)],
        compiler_params=_params(), name="shared_kv",
    )(h, g, w)


def _mem_kv(mem, g, w):
    nb = mem.shape[0]
    f32_spec = pl.BlockSpec((1, 1, N_MEM, MEM_WIDTH), lambda l, b: (l, b, 0, 0))
    return pl.pallas_call(
        _mem_kv_kernel,
        grid=(DEPTH, nb),
        in_specs=[pl.BlockSpec((1, N_MEM, D_MODEL), lambda l, b: (b, 0, 0)),
                  pl.BlockSpec((1, 1, D_MODEL), lambda l, b: (l, 0, 0)),
                  pl.BlockSpec((1, D_MODEL, 2 * MEM_WIDTH), lambda l, b: (l, 0, 0))],
        out_specs=[f32_spec, f32_spec,
                   pl.BlockSpec((1, 1, MEM_WIDTH, N_MEM), lambda l, b: (l, b, 0, 0)),
                   f32_spec],
        out_shape=[jax.ShapeDtypeStruct((DEPTH, nb, N_MEM, MEM_WIDTH), F32),
                   jax.ShapeDtypeStruct((DEPTH, nb, N_MEM, MEM_WIDTH), F32),
                   jax.ShapeDtypeStruct((DEPTH, nb, MEM_WIDTH, N_MEM), BF16),
                   jax.ShapeDtypeStruct((DEPTH, nb, N_MEM, MEM_WIDTH), BF16)],
        compiler_params=_params(), name="mem_kv",
    )(mem, g, w)


def _band_bias(rel_bias, q_chunks, k_chunks, first_tile_prev):
    nq, nk = q_chunks * CHUNK, k_chunks * CHUNK
    period = nq + nk
    d = np.arange(period)
    d = np.where(d < nk, d, d - period)
    idx = np.clip(B_WINDOW - d, -REL_CLIP, REL_CLIP) + REL_CLIP
    e = rel_bias.astype(F32)[..., idx]
    lead = e.shape[:-1]
    flat = jnp.tile(e, (1,) * len(lead) + (nq,))[..., :nq * (period - 1)]
    toep = flat.reshape(lead + (nq, period - 1))[..., :nk]
    dc = np.arange(nk)[None, :] // CHUNK - np.arange(nq)[:, None] // CHUNK
    valid = (dc >= 0) & (dc <= B_LEFT_CHUNKS)
    variants = [valid] + [valid & (np.arange(nk)[None, :] >= n_prev) for n_prev in first_tile_prev]
    return jnp.stack([jnp.where(m, toep, NEG_INF) for m in variants], axis=-4)


def _trunk(h, tile, prompt, mkT, mv, conv_prev, cache_kT, cache_v, wts):
    a_v, states = [], []
    k_last = v_last = None
    ffn_seqs = max(1, PROMPT_TILE // tile)
    for l in range(DEPTH):
        if l < N_A_LAYERS:
            h, v_rows = _mixer_a(h, tile, l, wts, mkT, mv, emit_v=not prompt)
            if not prompt:
                a_v.append(v_rows)
        else:
            if l == N_A_LAYERS:
                kT, v_bf, k_last, v_last = _shared_kv(h, tile, wts["g_kv"], wts["w_kv"])
            if prompt:
                blocks = tuple((qb * Q_BLOCK, qb * Q_BLOCK, 1 + qb) for qb in range(tile // Q_BLOCK))
                h = _mixer_b(h, tile, l, wts, kT, kT, v_bf, v_bf, wts["bias_prompt"], blocks, mkT, mv, True)
            else:
                h = _mixer_b(h, tile, l, wts, cache_kT, kT, cache_v, v_bf, wts["bias_sample"], ((0, 0, 0),),
                             mkT, mv, False)
        h, st = _ffn(h, ffn_seqs, tile, l, wts, conv_prev, final_norm=(l == DEPTH - 1))
        states.append(st)
    return h, a_v, k_last, v_last, jnp.stack(states)


def kernel(x_prompt, x_sample, mem_prompt, cache_mem_k, cache_mem_v, cache_b_k, cache_b_v, state_conv,
           g_mix, w_in_a, g_v, w_s, b_s, g_kv, w_kv, w_in_b, rel_bias, g_mem, w_mem_kv,
           w_out, g_ffn, w_up, conv_w, conv_b, w_down, g_final):
    bsz = x_prompt.shape[0]
    dbs = x_sample.shape[0]
    dec_seq = x_sample.shape[1]
    wts = {
        "g_mix": g_mix.reshape(DEPTH, 1, D_MODEL),
        "w_in_a": w_in_a.astype(BF16),
        "g_v": g_v.reshape(N_A_LAYERS, 1, TOK_WIDTH),
        "w_s": w_s,
        "bs_full": jnp.repeat(b_s.transpose(0, 2, 1), TOK_WIDTH // A_GROUPS, axis=-1),
        "g_kv": g_kv.reshape(1, D_MODEL),
        "w_kv": w_kv.astype(BF16),
        "w_in_b": w_in_b.astype(BF16),
        "bias_prompt": _band_bias(rel_bias, Q_BLOCK // CHUNK, Q_BLOCK // CHUNK + B_LEFT_CHUNKS,
                                  tuple(B_WINDOW - qb * Q_BLOCK for qb in range(PROMPT_TILE // Q_BLOCK))),
        "bias_sample": _band_bias(rel_bias, 1, 1 + B_LEFT_CHUNKS, ()),
        "w_out": w_out.astype(BF16),
        "g_ffn": g_ffn.reshape(DEPTH, 1, D_MODEL),
        "w_up": w_up.astype(BF16),
        "conv_w": conv_w,
        "conv_b": conv_b.reshape(DEPTH, 1, D_FF),
        "w_down": w_down.astype(BF16),
        "g_final": g_final.reshape(1, D_MODEL),
    }

    mem_k_p, mem_v_p, mkT_p, mv_p = _mem_kv(mem_prompt, g_mem.reshape(DEPTH, 1, D_MODEL),
                                            w_mem_kv.astype(BF16))
    conv0 = jnp.zeros((DEPTH, bsz, CONV_W - 1, D_FF), F32)
    y_p, _, k_p, v_p, conv_p = _trunk(x_prompt, PROMPT_TILE, True, mkT_p, mv_p, conv0, None, None, wts)

    mkT_s = cache_mem_k.reshape(DEPTH, dbs, N_MEM, MEM_WIDTH).transpose(0, 1, 3, 2).astype(BF16)
    mv_s = cache_mem_v.reshape(DEPTH, dbs, N_MEM, MEM_WIDTH).astype(BF16)
    cache_kT = cache_b_k.reshape(dbs, -1, TOK_WIDTH).transpose(0, 2, 1).astype(BF16)
    cache_v = cache_b_v.reshape(dbs, -1, TOK_WIDTH).astype(BF16)
    y_s, a_v_s, k_s, v_s, conv_s = _trunk(x_sample, dec_seq, False, mkT_s, mv_s, state_conv,
                                          cache_kT, cache_v, wts)

    hd = (B_HEADS, B_HEAD_DIM)
    mh = (MEM_HEADS, MEM_HEAD_DIM)
    return (y_p, y_s,
            mem_k_p.reshape(DEPTH, bsz, N_MEM, *mh), mem_v_p.reshape(DEPTH, bsz, N_MEM, *mh),
            k_p.reshape(bsz, B_WINDOW, *hd), v_p.reshape(bsz, B_WINDOW, *hd),
            conv_p,
            jnp.stack(a_v_s),
            k_s.reshape(dbs, dec_seq, *hd), v_s.reshape(dbs, dec_seq, *hd),
            conv_s)
```

```python
import functools

import numpy as np
import jax
import jax.numpy as jnp
from jax import lax
from jax.experimental import pallas as pl
from jax.experimental.pallas import tpu as pltpu

D_MODEL = 1024
DEPTH = 4
N_A_LAYERS = 2
TOK_WIDTH = 512
MEM_WIDTH = 512
A_CHUNK = 128
A_GROUPS = 8
B_HEADS = 8
B_HEAD_DIM = 64
CHUNK = 64
B_LEFT_CHUNKS = 8
B_WINDOW = B_LEFT_CHUNKS * CHUNK
REL_CLIP = 256
N_MEM = 256
MEM_HEADS = 4
MEM_HEAD_DIM = 128
D_FF = 2816
CONV_W = 3
EPS = 1e-6
NEG_INF = -1e30

LANES = 128
SUBLANES = 8
PROMPT_TILE = 512
FFN_TILE = 1024
WIDE_TILE = 1024
FFN_SAMPLE_ROWS = 512
FF_CHUNK = 256
N_FF_CHUNKS = D_FF // FF_CHUNK
FF_SLOTS = 3
ATT_AHEAD = 3
ATT_SLOTS = ATT_AHEAD + 1
FF_SPLIT = 6 * FF_CHUNK
Q_BLOCK = 256
VMEM_LIMIT = 56 * 1024 * 1024

F32 = jnp.float32
BF16 = jnp.bfloat16


def _rms(x, g):
    ms = jnp.mean(x * x, axis=-1, keepdims=True)
    return x * lax.rsqrt(ms + EPS) * g


def _softmax_parts(s):
    m = jnp.max(s, axis=-1, keepdims=True)
    p = jnp.exp(s - m)
    return p, jnp.sum(p, axis=-1, keepdims=True)


def _run_skewed(stages, ahead):
    n = len(stages)
    for t in range(n + ahead):
        if t < n:
            stages[t][0]()
        if t >= ahead:
            stages[t - ahead][1]()


def _mem_stages(z_ref, col0, mkT_ref, mv_ref, s_ref, cat_ref, first_index):
    stages = []
    for hh in range(MEM_HEADS):
        sl = slice(hh * MEM_HEAD_DIM, (hh + 1) * MEM_HEAD_DIM)
        slot = (first_index + hh) % s_ref.shape[0]

        def score(sl=sl, slot=slot):
            q = (z_ref[:, col0 + sl.start:col0 + sl.stop] * (MEM_HEAD_DIM ** -0.5)).astype(BF16)
            s_ref[slot] = jnp.dot(q, mkT_ref[0, sl, :], preferred_element_type=F32)

        def finish(sl=sl, slot=slot):
            p, l = _softmax_parts(s_ref[slot])
            o = jnp.dot(p.astype(BF16), mv_ref[0, :, sl], preferred_element_type=F32)
            cat_ref[:, TOK_WIDTH + sl.start:TOK_WIDTH + sl.stop] = (o * (1.0 / l)).astype(BF16)

        stages.append((score, finish))
    return stages


def _spatial_gate(u, v, ws_ref, bs_ref, chunk):
    rows = u.shape[0]
    r = lax.broadcasted_iota(jnp.int32, (A_CHUNK, A_CHUNK), 0)
    c = lax.broadcasted_iota(jnp.int32, (A_CHUNK, A_CHUNK), 1)
    w_tril = [jnp.where(r >= c, ws_ref[g], 0.0).astype(BF16)[:chunk, :] for g in range(A_GROUPS)]
    w_pair = [jnp.concatenate([w_tril[2 * pr], w_tril[2 * pr + 1]], axis=1) for pr in range(A_GROUPS // 2)]
    n_chunks = rows // chunk
    per_dot = 2 if n_chunks % 2 == 0 else 1
    lane = lax.broadcasted_iota(jnp.int32, (A_CHUNK, per_dot * LANES), 1)
    low = (lane & (LANES // 2)) == 0
    bias = bs_ref[:chunk, :]
    parts = [[None] * (A_GROUPS // 2) for _ in range(n_chunks)]
    for c0 in range(0, n_chunks, per_dot):
        vcs = [v[ci * chunk:(ci + 1) * chunk] for ci in range(c0, c0 + per_dot)]
        if chunk < A_CHUNK:
            vcs = [jnp.concatenate([vc, jnp.zeros((A_CHUNK - chunk, TOK_WIDTH), F32)], axis=0) for vc in vcs]
        for pr in range(A_GROUPS // 2):
            blocks = [vc[:, pr * LANES:(pr + 1) * LANES] for vc in vcs]
            vp = jnp.concatenate(blocks, axis=1) if per_dot > 1 else blocks[0]
            rhs = jnp.concatenate([jnp.where(low, vp, 0.0), jnp.where(low, 0.0, vp)], axis=0).astype(BF16)
            s = jnp.dot(w_pair[pr], rhs, preferred_element_type=F32)
            for k in range(per_dot):
                parts[c0 + k][pr] = s[:, k * LANES:(k + 1) * LANES]
    toks = [u[ci * chunk:(ci + 1) * chunk] * (jnp.concatenate(parts[ci], axis=-1) + bias)
            for ci in range(n_chunks)]
    return jnp.concatenate(toks, axis=0) if len(toks) > 1 else toks[0]


def _mixer_a_kernel(h_ref, gmix_ref, win_ref, gv_ref, ws_ref, bs_ref, mkT_ref, mv_ref, wout_ref,
                    *refs, chunk, emit_v):
    out_ref = refs[0]
    z_ref, ms_ref, cat_ref = refs[-3:]
    x = h_ref[0]
    xn = _rms(x, gmix_ref[...]).astype(BF16)
    z_ref[...] = jnp.dot(xn, win_ref[...], preferred_element_type=F32)
    stages = _mem_stages(z_ref, 2 * TOK_WIDTH, mkT_ref, mv_ref, ms_ref, cat_ref, 0)
    for score, _ in stages:
        score()
    uv = jax.nn.gelu(z_ref[:, :2 * TOK_WIDTH])
    u = uv[:, :TOK_WIDTH]
    v = _rms(uv[:, TOK_WIDTH:], gv_ref[...])
    if emit_v:
        refs[1][0] = v
    cat_ref[:, :TOK_WIDTH] = _spatial_gate(u, v, ws_ref, bs_ref, chunk).astype(BF16)
    for _, finish in stages:
        finish()
    out_ref[0] = x + jnp.dot(cat_ref[...], wout_ref[...], preferred_element_type=F32)


def _band_stages(z_ref, kT_win, v_win, bias_ref, blocks, first_tile, s_ref, cat_ref):
    qb, kb = bias_ref.shape[2], bias_ref.shape[3]
    lane_q = lax.broadcasted_iota(jnp.int32, (qb, LANES), 1)
    low = lane_q < B_HEAD_DIM
    stages = []
    held = {}
    for (row0, key0, first_variant) in blocks:
        variant = jnp.where(first_tile, first_variant, 0) if first_variant else 0
        for pr in range(B_HEADS // 2):
            psl = slice(pr * LANES, (pr + 1) * LANES)
            for half in range(2):
                slot = len(stages) % s_ref.shape[0]

                def score(row0=row0, key0=key0, variant=variant, psl=psl, pr=pr, half=half, slot=slot):
                    qp = z_ref[row0:row0 + qb, psl] * (B_HEAD_DIM ** -0.5)
                    qh = (jnp.where(low, qp, 0.0) if half == 0 else jnp.where(low, 0.0, qp)).astype(BF16)
                    s = jnp.dot(qh, kT_win[psl, key0:key0 + kb], preferred_element_type=F32)
                    s_ref[slot] = s + bias_ref[variant, 2 * pr + half]

                def finish(row0=row0, key0=key0, psl=psl, half=half, slot=slot):
                    p, l = _softmax_parts(s_ref[slot])
                    o = jnp.dot(p.astype(BF16), v_win[key0:key0 + kb, psl], preferred_element_type=F32)
                    o = o * (1.0 / l)
                    if half == 0:
                        held[(row0, psl.start)] = o
                    else:
                        pair = jnp.where(low, held.pop((row0, psl.start)), o)
                        cat_ref[row0:row0 + qb, psl] = pair.astype(BF16)

                stages.append((score, finish))
    return stages


def _mixer_b_kernel(h_ref, gmix_ref, win_ref, kTa_ref, kTb_ref, va_ref, vb_ref, bias_ref,
                    mkT_ref, mv_ref, wout_ref, out_ref, z_ref, s_ref, ms_ref, cat_ref, *, blocks):
    x = h_ref[0]
    xn = _rms(x, gmix_ref[...]).astype(BF16)
    z_ref[...] = jnp.dot(xn, win_ref[...], preferred_element_type=F32)
    kT_win = jnp.concatenate([kTa_ref[0], kTb_ref[0]], axis=1)
    v_win = jnp.concatenate([va_ref[0], vb_ref[0]], axis=0)
    stages = _band_stages(z_ref, kT_win, v_win, bias_ref, blocks, pl.program_id(1) == 0, s_ref, cat_ref)
    stages += _mem_stages(z_ref, TOK_WIDTH, mkT_ref, mv_ref, ms_ref, cat_ref, len(stages))
    _run_skewed(stages, ATT_AHEAD)
    out_ref[0] = x + jnp.dot(cat_ref[...], wout_ref[...], preferred_element_type=F32)


def _ffn_kernel(h_ref, g_ref, wup_ref, cw_ref, cb_ref, wd_ref, prev_ref, gfin_ref,
                out_ref, state_ref, carry_ref, xn_ref, a_ref, gate_ref, act_ref, *, final_norm):
    @pl.when(pl.program_id(1) == 0)
    def _():
        carry_ref[...] = prev_ref[...]

    n_seq, seq_len, _ = h_ref.shape
    rows = n_seq * seq_len
    x = h_ref[...].reshape(rows, D_MODEL)
    xn_ref[...] = _rms(x, g_ref[...]).astype(BF16)
    head_row = lax.broadcasted_iota(jnp.int32, (SUBLANES, FF_CHUNK), 0)

    for f in range(N_FF_CHUNKS):
        slot = f % FF_SLOTS
        fsl = slice(f * FF_CHUNK, (f + 1) * FF_CHUNK)
        a_ref[slot] = jnp.dot(xn_ref[...], wup_ref[:, fsl], preferred_element_type=F32)
        gate_ref[slot] = jnp.dot(xn_ref[...], wup_ref[:, D_FF + f * FF_CHUNK:D_FF + (f + 1) * FF_CHUNK],
                                 preferred_element_type=F32)
        a = a_ref[slot]
        gate = gate_ref[slot]
        w0, w1, w2 = cw_ref[0:1, fsl], cw_ref[1:2, fsl], cw_ref[2:3, fsl]
        ys = []
        for s in range(n_seq):
            a_s = a[s * seq_len:(s + 1) * seq_len]
            p0, p1 = carry_ref[s, 0:1, fsl], carry_ref[s, 1:2, fsl]
            r1, r2 = pltpu.roll(a_s, 1, 0), pltpu.roll(a_s, 2, 0)
            a1 = jnp.concatenate([jnp.where(head_row == 0, p1, r1[:SUBLANES]), r1[SUBLANES:]], axis=0)
            a2 = jnp.concatenate([jnp.where(head_row == 0, p0, jnp.where(head_row == 1, p1, r2[:SUBLANES])),
                                  r2[SUBLANES:]], axis=0)
            carry_ref[s, :, fsl] = a_s[seq_len - 2:seq_len]
            ys.append(cb_ref[:, fsl] + w0 * a2 + w1 * a1 + w2 * a_s)
        y = jnp.concatenate(ys, axis=0) if n_seq > 1 else ys[0]
        act_ref[:, fsl] = (jax.nn.gelu(y) * gate).astype(BF16)

    state_ref[...] = carry_ref[...]
    hn = x
    for lo, hi in ((0, FF_SPLIT), (FF_SPLIT, D_FF)):
        hn = hn + jnp.dot(act_ref[:, lo:hi], wd_ref[lo:hi, :], preferred_element_type=F32)
    out = _rms(hn, gfin_ref[...]) if final_norm else hn
    out_ref[...] = out.reshape(n_seq, seq_len, D_MODEL)


def _kv_kernel(h_ref, g_ref, w_ref, kT_ref, v_ref, kf_ref, vf_ref):
    hn = _rms(h_ref[0], g_ref[...]).astype(BF16)
    kv = jnp.dot(hn, w_ref[...], preferred_element_type=F32)
    k, v = kv[:, :TOK_WIDTH], kv[:, TOK_WIDTH:]
    last = kf_ref.shape[1]
    kf_ref[0] = k[k.shape[0] - last:]
    vf_ref[0] = v[v.shape[0] - last:]
    kT_ref[0] = k.T.astype(BF16)
    v_ref[0] = v.astype(BF16)


def _mem_kv_kernel(m_ref, g_ref, w_ref, k_ref, v_ref, kT_ref, vb_ref):
    mn = _rms(m_ref[0], g_ref[0]).astype(BF16)
    kv = jnp.dot(mn, w_ref[0], preferred_element_type=F32)
    k, v = kv[:, :MEM_WIDTH], kv[:, MEM_WIDTH:]
    k_ref[0, 0] = k.reshape(N_MEM, MEM_HEADS, MEM_HEAD_DIM)
    v_ref[0, 0] = v.reshape(N_MEM, MEM_HEADS, MEM_HEAD_DIM)
    kT_ref[0, 0] = k.T.astype(BF16)
    vb_ref[0, 0] = v.astype(BF16)


def _const_spec(shape):
    nd = len(shape)
    return pl.BlockSpec(shape, lambda b, i: (0,) * nd, pipeline_mode=pl.Buffered(1))


def _layer_spec(stacked, l):
    nd = stacked.ndim
    return pl.BlockSpec((None,) + stacked.shape[1:], lambda b, i: (l,) + (0,) * (nd - 1),
                        pipeline_mode=pl.Buffered(1))


def _mem_specs(l):
    return [pl.BlockSpec((None, 1, MEM_WIDTH, N_MEM), lambda b, i: (l, b, 0, 0)),
            pl.BlockSpec((None, 1, N_MEM, MEM_WIDTH), lambda b, i: (l, b, 0, 0))]


def _params():
    return pltpu.CompilerParams(dimension_semantics=("arbitrary", "arbitrary"),
                                vmem_limit_bytes=VMEM_LIMIT)


def _mixer_a(h, tile, l, wts, mk, mv, emit_v):
    nb, seq, _ = h.shape
    chunk = min(A_CHUNK, tile)
    act_spec = pl.BlockSpec((1, tile, D_MODEL), lambda b, i: (b, i, 0))
    out_shape = [jax.ShapeDtypeStruct(h.shape, F32)]
    out_specs = [act_spec]
    if emit_v:
        out_shape.append(jax.ShapeDtypeStruct((nb, seq, TOK_WIDTH), F32))
        out_specs.append(pl.BlockSpec((1, tile, TOK_WIDTH), lambda b, i: (b, i, 0)))
    stacked = [wts[k] for k in ("g_mix", "w_in_a", "g_v", "w_s", "bs_full")]
    res = pl.pallas_call(
        functools.partial(_mixer_a_kernel, chunk=chunk, emit_v=emit_v),
        grid=(nb, seq // tile),
        in_specs=[act_spec] + [_layer_spec(w, l) for w in stacked] + _mem_specs(l)
        + [_layer_spec(wts["w_out"], l)],
        out_specs=out_specs, out_shape=out_shape,
        scratch_shapes=[pltpu.VMEM((tile, 2 * TOK_WIDTH + MEM_WIDTH), F32),
                        pltpu.VMEM((MEM_HEADS, tile, N_MEM), F32),
                        pltpu.VMEM((tile, D_MODEL), BF16)],
        compiler_params=_params(), name="mixer_a",
    )(h, *stacked, mk, mv, wts["w_out"])
    return res if emit_v else (res[0], None)


def _mixer_b(h, tile, l, wts, kT_a, kT_b, v_a, v_b, bias, blocks, mk, mv, prompt):
    nb, seq, _ = h.shape
    j = l - N_A_LAYERS
    act_spec = pl.BlockSpec((1, tile, D_MODEL), lambda b, i: (b, i, 0))
    if prompt:
        ka = pl.BlockSpec((1, TOK_WIDTH, tile), lambda b, i: (b, 0, jnp.maximum(i - 1, 0)))
        kb = pl.BlockSpec((1, TOK_WIDTH, tile), lambda b, i: (b, 0, i))
        va = pl.BlockSpec((1, tile, TOK_WIDTH), lambda b, i: (b, jnp.maximum(i - 1, 0), 0))
        vb = pl.BlockSpec((1, tile, TOK_WIDTH), lambda b, i: (b, i, 0))
    else:
        ka = pl.BlockSpec((1,) + kT_a.shape[1:], lambda b, i: (b, 0, 0))
        kb = pl.BlockSpec((1,) + kT_b.shape[1:], lambda b, i: (b, 0, 0))
        va = pl.BlockSpec((1,) + v_a.shape[1:], lambda b, i: (b, 0, 0))
        vb = pl.BlockSpec((1,) + v_b.shape[1:], lambda b, i: (b, 0, 0))
    return pl.pallas_call(
        functools.partial(_mixer_b_kernel, blocks=blocks),
        grid=(nb, seq // tile),
        in_specs=[act_spec, _layer_spec(wts["g_mix"], l), _layer_spec(wts["w_in_b"], j), ka, kb, va, vb,
                  _layer_spec(bias, j)] + _mem_specs(l) + [_layer_spec(wts["w_out"], l)],
        out_specs=act_spec, out_shape=jax.ShapeDtypeStruct(h.shape, F32),
        scratch_shapes=[pltpu.VMEM((tile, TOK_WIDTH + MEM_WIDTH), F32),
                        pltpu.VMEM((ATT_SLOTS,) + bias.shape[3:], F32),
                        pltpu.VMEM((ATT_SLOTS, tile, N_MEM), F32),
                        pltpu.VMEM((tile, D_MODEL), BF16)],
        compiler_params=_params(), name="mixer_b",
    )(h, wts["g_mix"], wts["w_in_b"], kT_a, kT_b, v_a, v_b, bias, mk, mv, wts["w_out"])


def _ffn(h, n_seq, tile, l, wts, prev, final_norm):
    nb, seq, _ = h.shape
    act_spec = pl.BlockSpec((n_seq, tile, D_MODEL), lambda b, i: (b, i, 0))
    stacked = [wts[k] for k in ("g_ffn", "w_up", "conv_w", "conv_b", "w_down")]
    return pl.pallas_call(
        functools.partial(_ffn_kernel, final_norm=final_norm),
        grid=(nb // n_seq, seq // tile),
        in_specs=[act_spec] + [_layer_spec(w, l) for w in stacked]
        + [pl.BlockSpec((None, n_seq, CONV_W - 1, D_FF), lambda b, i: (l, b, 0, 0)),
           _const_spec(wts["g_final"].shape)],
        out_specs=[act_spec, pl.BlockSpec((n_seq, CONV_W - 1, D_FF), lambda b, i: (b, 0, 0))],
        out_shape=[jax.ShapeDtypeStruct(h.shape, F32),
                   jax.ShapeDtypeStruct((nb, CONV_W - 1, D_FF), F32)],
        scratch_shapes=[pltpu.VMEM((n_seq, CONV_W - 1, D_FF), F32),
                        pltpu.VMEM((n_seq * tile, D_MODEL), BF16),
                        pltpu.VMEM((FF_SLOTS, n_seq * tile, FF_CHUNK), F32),
                        pltpu.VMEM((FF_SLOTS, n_seq * tile, FF_CHUNK), F32),
                        pltpu.VMEM((n_seq * tile, D_FF), BF16)],
        compiler_params=_params(), name="conv_ffn",
    )(h, *stacked, prev, wts["g_final"])


def _shared_kv(h, tile, g, w):
    nb, seq, _ = h.shape
    n_t = seq // tile
    last = min(tile, B_WINDOW)
    act_spec = pl.BlockSpec((1, tile, D_MODEL), lambda b, i: (b, i, 0))
    last_spec = pl.BlockSpec((1, last, TOK_WIDTH), lambda b, i: (b, 0, 0))
    return pl.pallas_call(
        _kv_kernel,
        grid=(nb, n_t),
        in_specs=[act_spec, _const_spec(g.shape), _const_spec(w.shape)],
        out_specs=[pl.BlockSpec((1, TOK_WIDTH, tile), lambda b, i: (b, 0, i)),
                   pl.BlockSpec((1, tile, TOK_WIDTH), lambda b, i: (b, i, 0)),
                   last_spec, last_spec],
        out_shape=[jax.ShapeDtypeStruct((nb, TOK_WIDTH, seq), BF16),
                   jax.ShapeDtypeStruct((nb, seq, TOK_WIDTH), BF16),
                   jax.ShapeDtypeStruct((nb, last, TOK_WIDTH), F32),
                   jax.ShapeDtypeStruct((nb, last, TOK_WIDTH), F32)],
        compiler_params=_params(), name="shared_kv",
    )(h, g, w)


def _mem_kv(mem, g, w):
    nb = mem.shape[0]
    out_spec = pl.BlockSpec((1, 1, N_MEM, MEM_HEADS, MEM_HEAD_DIM), lambda l, b: (l, b, 0, 0, 0))
    out_shape = jax.ShapeDtypeStruct((DEPTH, nb, N_MEM, MEM_HEADS, MEM_HEAD_DIM), F32)
    return pl.pallas_call(
        _mem_kv_kernel,
        grid=(DEPTH, nb),
        in_specs=[pl.BlockSpec((1, N_MEM, D_MODEL), lambda l, b: (b, 0, 0)),
                  pl.BlockSpec((1, 1, D_MODEL), lambda l, b: (l, 0, 0)),
                  pl.BlockSpec((1, D_MODEL, 2 * MEM_WIDTH), lambda l, b: (l, 0, 0))],
        out_specs=[out_spec, out_spec,
                   pl.BlockSpec((1, 1, MEM_WIDTH, N_MEM), lambda l, b: (l, b, 0, 0)),
                   pl.BlockSpec((1, 1, N_MEM, MEM_WIDTH), lambda l, b: (l, b, 0, 0))],
        out_shape=[out_shape, out_shape,
                   jax.ShapeDtypeStruct((DEPTH, nb, MEM_WIDTH, N_MEM), BF16),
                   jax.ShapeDtypeStruct((DEPTH, nb, N_MEM, MEM_WIDTH), BF16)],
        compiler_params=_params(), name="mem_kv",
    )(mem, g, w)


def _band_bias(rel_bias, q_chunks, k_chunks, first_tile_prev):
    nq, nk = q_chunks * CHUNK, k_chunks * CHUNK
    period = nq + nk
    d = np.arange(period)
    d = np.where(d < nk, d, d - period)
    idx = np.clip(B_WINDOW - d, -REL_CLIP, REL_CLIP) + REL_CLIP
    e = rel_bias.astype(F32)[..., idx]
    lead = e.shape[:-1]
    flat = jnp.tile(e, (1,) * len(lead) + (nq,))[..., :nq * (period - 1)]
    toep = flat.reshape(lead + (nq, period - 1))[..., :nk]
    dc = np.arange(nk)[None, :] // CHUNK - np.arange(nq)[:, None] // CHUNK
    valid = (dc >= 0) & (dc <= B_LEFT_CHUNKS)
    variants = [valid] + [valid & (np.arange(nk)[None, :] >= n_prev) for n_prev in first_tile_prev]
    return jnp.stack([jnp.where(m, toep, NEG_INF) for m in variants], axis=-4)


def _trunk(h, tile, prompt, mk, mv, conv_prev, cache_kT, cache_v, wts):
    a_v, states = [], []
    k_last = v_last = None
    ffn_tile = FFN_TILE if prompt else tile
    ffn_seqs = 1 if prompt else FFN_SAMPLE_ROWS // tile
    wide_tile = WIDE_TILE if prompt else tile
    for l in range(DEPTH):
        if l < N_A_LAYERS:
            h, v_rows = _mixer_a(h, wide_tile, l, wts, mk, mv, emit_v=not prompt)
            if not prompt:
                a_v.append(v_rows)
        else:
            if l == N_A_LAYERS:
                kT, v_bf, k_last, v_last = _shared_kv(h, wide_tile, wts["g_kv"], wts["w_kv"])
            if prompt:
                blocks = tuple((qb * Q_BLOCK, qb * Q_BLOCK, 1 + qb) for qb in range(tile // Q_BLOCK))
                h = _mixer_b(h, tile, l, wts, kT, kT, v_bf, v_bf, wts["bias_prompt"], blocks, mk, mv, True)
            else:
                h = _mixer_b(h, tile, l, wts, cache_kT, kT, cache_v, v_bf, wts["bias_sample"], ((0, 0, 0),),
                             mk, mv, False)
        h, st = _ffn(h, ffn_seqs, ffn_tile, l, wts, conv_prev, final_norm=(l == DEPTH - 1))
        states.append(st)
    return h, a_v, k_last, v_last, jnp.stack(states)


def kernel(x_prompt, x_sample, mem_prompt, cache_mem_k, cache_mem_v, cache_b_k, cache_b_v, state_conv,
           g_mix, w_in_a, g_v, w_s, b_s, g_kv, w_kv, w_in_b, rel_bias, g_mem, w_mem_kv,
           w_out, g_ffn, w_up, conv_w, conv_b, w_down, g_final):
    bsz = x_prompt.shape[0]
    dbs = x_sample.shape[0]
    dec_seq = x_sample.shape[1]
    wts = {
        "g_mix": g_mix.reshape(DEPTH, 1, D_MODEL),
        "w_in_a": w_in_a.astype(BF16),
        "g_v": g_v.reshape(N_A_LAYERS, 1, TOK_WIDTH),
        "w_s": w_s,
        "bs_full": jnp.repeat(b_s.transpose(0, 2, 1), TOK_WIDTH // A_GROUPS, axis=-1),
        "g_kv": g_kv.reshape(1, D_MODEL),
        "w_kv": w_kv.astype(BF16),
        "w_in_b": w_in_b.astype(BF16),
        "bias_prompt": _band_bias(rel_bias, Q_BLOCK // CHUNK, Q_BLOCK // CHUNK + B_LEFT_CHUNKS,
                                  tuple(B_WINDOW - qb * Q_BLOCK for qb in range(PROMPT_TILE // Q_BLOCK))),
        "bias_sample": _band_bias(rel_bias, 1, 1 + B_LEFT_CHUNKS, ()),
        "w_out": w_out.astype(BF16),
        "g_ffn": g_ffn.reshape(DEPTH, 1, D_MODEL),
        "w_up": w_up.astype(BF16),
        "conv_w": conv_w,
        "conv_b": conv_b.reshape(DEPTH, 1, D_FF),
        "w_down": w_down.astype(BF16),
        "g_final": g_final.reshape(1, D_MODEL),
    }

    mem_k_p, mem_v_p, mkT_p, mv_p = _mem_kv(mem_prompt, g_mem.reshape(DEPTH, 1, D_MODEL),
                                            w_mem_kv.astype(BF16))
    conv0 = jnp.zeros((DEPTH, bsz, CONV_W - 1, D_FF), F32)
    y_p, _, k_p, v_p, conv_p = _trunk(x_prompt, PROMPT_TILE, True, mkT_p, mv_p, conv0, None, None, wts)

    mkT_s = cache_mem_k.reshape(DEPTH, dbs, N_MEM, MEM_WIDTH).transpose(0, 1, 3, 2).astype(BF16)
    mv_s = cache_mem_v.reshape(DEPTH, dbs, N_MEM, MEM_WIDTH).astype(BF16)
    cache_kT = cache_b_k.reshape(dbs, -1, TOK_WIDTH).transpose(0, 2, 1).astype(BF16)
    cache_v = cache_b_v.reshape(dbs, -1, TOK_WIDTH).astype(BF16)
    y_s, a_v_s, k_s, v_s, conv_s = _trunk(x_sample, dec_seq, False, mkT_s, mv_s, state_conv,
                                          cache_kT, cache_v, wts)

    hd = (B_HEADS, B_HEAD_DIM)
    return (y_p, y_s,
            mem_k_p, mem_v_p,
            k_p.reshape(bsz, B_WINDOW, *hd), v_p.reshape(bsz, B_WINDOW, *hd),
            conv_p,
            jnp.stack(a_v_s),
            k_s.reshape(dbs, dec_seq, *hd), v_s.reshape(dbs, dec_seq, *hd),
            conv_s)
```

```python
import functools

import numpy as np
import jax
import jax.numpy as jnp
from jax import lax
from jax.experimental import pallas as pl
from jax.experimental.pallas import tpu as pltpu

D_MODEL = 1024
DEPTH = 4
N_A_LAYERS = 2
TOK_WIDTH = 512
MEM_WIDTH = 512
A_CHUNK = 128
A_GROUPS = 8
B_HEADS = 8
B_HEAD_DIM = 64
CHUNK = 64
B_LEFT_CHUNKS = 8
B_WINDOW = B_LEFT_CHUNKS * CHUNK
REL_CLIP = 256
N_MEM = 256
MEM_HEADS = 4
MEM_HEAD_DIM = 128
D_FF = 2816
CONV_W = 3
EPS = 1e-6
NEG_INF = -1e30

LANES = 128
SUBLANES = 8
PROMPT_TILE = 512
FFN_TILE = 1024
WIDE_TILE = 1024
FFN_SAMPLE_ROWS = 512
FF_CHUNK = 256
N_FF_CHUNKS = D_FF // FF_CHUNK
FF_SLOTS = 3
ATT_AHEAD = 3
ATT_SLOTS = ATT_AHEAD + 3
FF_SPLIT = 6 * FF_CHUNK
Q_BLOCK = 256
VMEM_LIMIT = 56 * 1024 * 1024

F32 = jnp.float32
BF16 = jnp.bfloat16


def _rms(x, g):
    ms = jnp.mean(x * x, axis=-1, keepdims=True)
    return x * lax.rsqrt(ms + EPS) * g


def _softmax_parts(s):
    m = jnp.max(s, axis=-1, keepdims=True)
    p = jnp.exp(s - m)
    return p, jnp.sum(p, axis=-1, keepdims=True)


def _run_skewed(stages, ahead):
    n = len(stages)
    for t in range(n + ahead):
        if t < n:
            stages[t][0]()
        if t >= ahead:
            stages[t - ahead][1]()


def _mem_stages(z_ref, col0, mkT_ref, mv_ref, s_refs, cat_ref, first_index):
    stages = []
    for hh in range(MEM_HEADS):
        sl = slice(hh * MEM_HEAD_DIM, (hh + 1) * MEM_HEAD_DIM)
        s_ref = s_refs[(first_index + hh) % len(s_refs)]

        def score(sl=sl, s_ref=s_ref):
            q = (z_ref[:, col0 + sl.start:col0 + sl.stop] * (MEM_HEAD_DIM ** -0.5)).astype(BF16)
            s_ref[...] = jnp.dot(q, mkT_ref[0, sl, :], preferred_element_type=F32)

        def finish(sl=sl, s_ref=s_ref):
            p, l = _softmax_parts(s_ref[...])
            o = jnp.dot(p.astype(BF16), mv_ref[0, :, sl], preferred_element_type=F32)
            cat_ref[:, TOK_WIDTH + sl.start:TOK_WIDTH + sl.stop] = (o * (1.0 / l)).astype(BF16)

        stages.append((score, finish))
    return stages


def _spatial_gate(u, v, ws_ref, bs_ref, chunk):
    rows = u.shape[0]
    r = lax.broadcasted_iota(jnp.int32, (A_CHUNK, A_CHUNK), 0)
    c = lax.broadcasted_iota(jnp.int32, (A_CHUNK, A_CHUNK), 1)
    w_tril = [jnp.where(r >= c, ws_ref[g], 0.0).astype(BF16)[:chunk, :] for g in range(A_GROUPS)]
    w_pair = [jnp.concatenate([w_tril[2 * pr], w_tril[2 * pr + 1]], axis=1) for pr in range(A_GROUPS // 2)]
    n_chunks = rows // chunk
    per_dot = 2 if n_chunks % 2 == 0 else 1
    lane = lax.broadcasted_iota(jnp.int32, (A_CHUNK, per_dot * LANES), 1)
    low = (lane & (LANES // 2)) == 0
    bias = bs_ref[:chunk, :]
    parts = [[None] * (A_GROUPS // 2) for _ in range(n_chunks)]
    for c0 in range(0, n_chunks, per_dot):
        vcs = [v[ci * chunk:(ci + 1) * chunk] for ci in range(c0, c0 + per_dot)]
        if chunk < A_CHUNK:
            vcs = [jnp.concatenate([vc, jnp.zeros((A_CHUNK - chunk, TOK_WIDTH), F32)], axis=0) for vc in vcs]
        for pr in range(A_GROUPS // 2):
            blocks = [vc[:, pr * LANES:(pr + 1) * LANES] for vc in vcs]
            vp = jnp.concatenate(blocks, axis=1) if per_dot > 1 else blocks[0]
            rhs = jnp.concatenate([jnp.where(low, vp, 0.0), jnp.where(low, 0.0, vp)], axis=0).astype(BF16)
            s = jnp.dot(w_pair[pr], rhs, preferred_element_type=F32)
            for k in range(per_dot):
                parts[c0 + k][pr] = s[:, k * LANES:(k + 1) * LANES]
    toks = [u[ci * chunk:(ci + 1) * chunk] * (jnp.concatenate(parts[ci], axis=-1) + bias)
            for ci in range(n_chunks)]
    return jnp.concatenate(toks, axis=0) if len(toks) > 1 else toks[0]


def _mixer_a_kernel(h_ref, gmix_ref, win_ref, gv_ref, ws_ref, bs_ref, mkT_ref, mv_ref, wout_ref,
                    *refs, chunk, emit_v):
    out_ref = refs[0]
    z_ref, cat_ref = refs[-MEM_HEADS - 2], refs[-1]
    ms_refs = refs[-MEM_HEADS - 1:-1]
    x = h_ref[0]
    xn = _rms(x, gmix_ref[...]).astype(BF16)
    z_ref[...] = jnp.dot(xn, win_ref[...], preferred_element_type=F32)
    stages = _mem_stages(z_ref, 2 * TOK_WIDTH, mkT_ref, mv_ref, ms_refs, cat_ref, 0)
    for score, _ in stages:
        score()
    uv = jax.nn.gelu(z_ref[:, :2 * TOK_WIDTH])
    u = uv[:, :TOK_WIDTH]
    v = _rms(uv[:, TOK_WIDTH:], gv_ref[...])
    if emit_v:
        refs[1][0] = v
    cat_ref[:, :TOK_WIDTH] = _spatial_gate(u, v, ws_ref, bs_ref, chunk).astype(BF16)
    for _, finish in stages:
        finish()
    out_ref[0] = x + jnp.dot(cat_ref[...], wout_ref[...], preferred_element_type=F32)


def _window(ref_a, ref_b, axis, other, lo, hi):
    na = ref_a.shape[1 + axis]
    parts = []
    if lo < na:
        sl = slice(lo, min(hi, na))
        parts.append(ref_a[0, sl, other] if axis == 0 else ref_a[0, other, sl])
    if hi > na:
        sl = slice(max(lo - na, 0), hi - na)
        parts.append(ref_b[0, sl, other] if axis == 0 else ref_b[0, other, sl])
    return parts[0] if len(parts) == 1 else jnp.concatenate(parts, axis=axis)


def _band_stages(z_ref, kT_refs, v_refs, bias_ref, blocks, first_tile, s_refs, m_refs, cat_ref):
    qb, kb = bias_ref.shape[2], bias_ref.shape[3]
    lane_q = lax.broadcasted_iota(jnp.int32, (qb, LANES), 1)
    low = lane_q < B_HEAD_DIM
    lane_k = lax.broadcasted_iota(jnp.int32, (kb, LANES), 1)
    low_k = lane_k < B_HEAD_DIM
    stages = []
    held = {}
    for (row0, key0, first_variant) in blocks:
        variant = jnp.where(first_tile, first_variant, 0) if first_variant else 0
        for pr in range(B_HEADS // 2):
            psl = slice(pr * LANES, (pr + 1) * LANES)
            for half in range(2):
                s_ref = s_refs[len(stages) % len(s_refs)]
                m_ref = m_refs[len(stages) % len(m_refs)]

                def score(row0=row0, key0=key0, variant=variant, psl=psl, pr=pr, half=half,
                          s_ref=s_ref, m_ref=m_ref):
                    qp = z_ref[row0:row0 + qb, psl] * (B_HEAD_DIM ** -0.5)
                    qh = (jnp.where(low, qp, 0.0) if half == 0 else jnp.where(low, 0.0, qp)).astype(BF16)
                    s = jnp.dot(qh, _window(*kT_refs, 1, psl, key0, key0 + kb), preferred_element_type=F32)
                    s = s + bias_ref[variant, 2 * pr + half]
                    s_ref[...] = s
                    m = s[:, :LANES]
                    for j in range(1, kb // LANES):
                        m = jnp.maximum(m, s[:, j * LANES:(j + 1) * LANES])
                    m_ref[...] = m

                def finish(row0=row0, key0=key0, psl=psl, half=half, s_ref=s_ref, m_ref=m_ref):
                    m = jnp.max(m_ref[...], axis=-1, keepdims=True)
                    p = jnp.exp(s_ref[...] - m).astype(BF16)
                    vv = _window(*v_refs, 0, psl, key0, key0 + kb)
                    vv = jnp.where(low_k, vv, 1.0) if half == 0 else jnp.where(low_k, 1.0, vv)
                    o = jnp.dot(p, vv.astype(BF16), preferred_element_type=F32)
                    o = o * (1.0 / pltpu.roll(o, B_HEAD_DIM, 1))
                    if half == 0:
                        held[(row0, psl.start)] = o
                    else:
                        pair = jnp.where(low, held.pop((row0, psl.start)), o)
                        cat_ref[row0:row0 + qb, psl] = pair.astype(BF16)

                stages.append((score, finish))
    return stages


def _mixer_b_kernel(h_ref, gmix_ref, win_ref, kTa_ref, kTb_ref, va_ref, vb_ref, bias_ref,
                    mkT_ref, mv_ref, wout_ref, out_ref, z_ref, *scratch, blocks):
    s_refs, m_refs, ms_refs = (scratch[k * ATT_SLOTS:(k + 1) * ATT_SLOTS] for k in range(3))
    cat_ref = scratch[-1]
    x = h_ref[0]
    xn = _rms(x, gmix_ref[...]).astype(BF16)
    z_ref[...] = jnp.dot(xn, win_ref[...], preferred_element_type=F32)
    stages = _band_stages(z_ref, (kTa_ref, kTb_ref), (va_ref, vb_ref), bias_ref, blocks, pl.program_id(1) == 0, s_refs, m_refs, cat_ref)
    stages += _mem_stages(z_ref, TOK_WIDTH, mkT_ref, mv_ref, ms_refs, cat_ref, len(stages))
    _run_skewed(stages, ATT_AHEAD)
    out_ref[0] = x + jnp.dot(cat_ref[...], wout_ref[...], preferred_element_type=F32)


def _band_stages_t(qT_ref, k_refs, vT_refs, biasT_ref, blocks, first_tile, s_refs, m_refs, cat_ref):
    kb, qb = biasT_ref.shape[2], biasT_ref.shape[3]
    top_v = lax.broadcasted_iota(jnp.int32, (LANES, kb), 0) < B_HEAD_DIM
    zeros = jnp.zeros((B_HEAD_DIM, qb), F32)
    stages = []
    held = {}
    for (row0, key0, first_variant) in blocks:
        variant = jnp.where(first_tile, first_variant, 0) if first_variant else 0
        for pr in range(B_HEADS // 2):
            psl = slice(pr * LANES, (pr + 1) * LANES)
            for half in range(2):
                s_ref = s_refs[len(stages) % len(s_refs)]
                m_ref = m_refs[len(stages) % len(m_refs)]

                def score(row0=row0, key0=key0, variant=variant, psl=psl, pr=pr, half=half,
                          s_ref=s_ref, m_ref=m_ref):
                    qp = qT_ref[psl, row0:row0 + qb]
                    qh = jnp.concatenate([qp[:B_HEAD_DIM], zeros] if half == 0 else [zeros, qp[B_HEAD_DIM:]], axis=0)
                    s = jnp.dot(_window(*k_refs, 0, psl, key0, key0 + kb), qh.astype(BF16),
                                preferred_element_type=F32)
                    s = s + biasT_ref[variant, 2 * pr + half]
                    s_ref[...] = s
                    m_ref[...] = jnp.max(s.reshape(kb // SUBLANES, SUBLANES, qb), axis=0)

                def finish(row0=row0, key0=key0, psl=psl, half=half, s_ref=s_ref, m_ref=m_ref):
                    m = jnp.max(m_ref[...], axis=0, keepdims=True)
                    p = jnp.exp(s_ref[...] - m).astype(BF16)
                    vT = _window(*vT_refs, 1, psl, key0, key0 + kb)
                    vT = jnp.where(top_v, vT, 1.0) if half == 0 else jnp.where(top_v, 1.0, vT)
                    o = jnp.dot(vT.astype(BF16), p, preferred_element_type=F32)
                    if half == 0:
                        held[(row0, psl.start)] = o[:B_HEAD_DIM] * (1.0 / o[B_HEAD_DIM:B_HEAD_DIM + 1])
                    else:
                        on = o[B_HEAD_DIM:] * (1.0 / o[0:1])
                        pair_t = jnp.concatenate([held.pop((row0, psl.start)), on], axis=0)
                        cat_ref[row0:row0 + qb, psl] = pair_t.T.astype(BF16)

                stages.append((score, finish))
    return stages


def _mixer_bt_kernel(h_ref, gmix_ref, wqT_ref, wmem_ref, ka_ref, kb_ref, vTa_ref, vTb_ref, biasT_ref,
                     mkT_ref, mv_ref, wout_ref, out_ref, qT_ref, z_ref, *scratch, blocks):
    s_refs, m_refs, ms_refs = (scratch[k * ATT_SLOTS:(k + 1) * ATT_SLOTS] for k in range(3))
    cat_ref = scratch[-1]
    x = h_ref[0]
    xn = _rms(x, gmix_ref[...]).astype(BF16)
    qT = lax.dot_general(wqT_ref[...], xn, (((1,), (1,)), ((), ())), preferred_element_type=F32)
    qT_ref[...] = qT * (B_HEAD_DIM ** -0.5)
    z_ref[...] = jnp.dot(xn, wmem_ref[...], preferred_element_type=F32)
    stages = _band_stages_t(qT_ref, (ka_ref, kb_ref), (vTa_ref, vTb_ref), biasT_ref, blocks,
                            pl.program_id(1) == 0, s_refs, m_refs, cat_ref)
    stages += _mem_stages(z_ref, 0, mkT_ref, mv_ref, ms_refs, cat_ref, len(stages))
    _run_skewed(stages, ATT_AHEAD)
    out_ref[0] = x + jnp.dot(cat_ref[...], wout_ref[...], preferred_element_type=F32)


def _ffn_kernel(h_ref, g_ref, wup_ref, cw_ref, cb_ref, wd_ref, prev_ref, gfin_ref,
                out_ref, state_ref, carry_ref, xn_ref, a_ref, gate_ref, act_ref, *, final_norm):
    @pl.when(pl.program_id(1) == 0)
    def _():
        carry_ref[...] = prev_ref[...]

    n_seq, seq_len, _ = h_ref.shape
    rows = n_seq * seq_len
    x = h_ref[...].reshape(rows, D_MODEL)
    xn_ref[...] = _rms(x, g_ref[...]).astype(BF16)
    head_row = lax.broadcasted_iota(jnp.int32, (SUBLANES, FF_CHUNK), 0)

    for f in range(N_FF_CHUNKS):
        slot = f % FF_SLOTS
        fsl = slice(f * FF_CHUNK, (f + 1) * FF_CHUNK)
        a_ref[slot] = jnp.dot(xn_ref[...], wup_ref[:, fsl], preferred_element_type=F32)
        gate_ref[slot] = jnp.dot(xn_ref[...], wup_ref[:, D_FF + f * FF_CHUNK:D_FF + (f + 1) * FF_CHUNK],
                                 preferred_element_type=F32)
        a = a_ref[slot]
        gate = gate_ref[slot]
        w0, w1, w2 = cw_ref[0:1, fsl], cw_ref[1:2, fsl], cw_ref[2:3, fsl]
        ys = []
        for s in range(n_seq):
            a_s = a[s * seq_len:(s + 1) * seq_len]
            p0, p1 = carry_ref[s, 0:1, fsl], carry_ref[s, 1:2, fsl]
            r1, r2 = pltpu.roll(a_s, 1, 0), pltpu.roll(a_s, 2, 0)
            a1 = jnp.concatenate([jnp.where(head_row == 0, p1, r1[:SUBLANES]), r1[SUBLANES:]], axis=0)
            a2 = jnp.concatenate([jnp.where(head_row == 0, p0, jnp.where(head_row == 1, p1, r2[:SUBLANES])),
                                  r2[SUBLANES:]], axis=0)
            carry_ref[s, :, fsl] = a_s[seq_len - 2:seq_len]
            ys.append(cb_ref[:, fsl] + w0 * a2 + w1 * a1 + w2 * a_s)
        y = jnp.concatenate(ys, axis=0) if n_seq > 1 else ys[0]
        act_ref[:, fsl] = (jax.nn.gelu(y) * gate).astype(BF16)

    state_ref[...] = carry_ref[...]
    hn = x
    for lo, hi in ((0, FF_SPLIT), (FF_SPLIT, D_FF)):
        hn = hn + jnp.dot(act_ref[:, lo:hi], wd_ref[lo:hi, :], preferred_element_type=F32)
    out = _rms(hn, gfin_ref[...]) if final_norm else hn
    out_ref[...] = out.reshape(n_seq, seq_len, D_MODEL)


def _kv_kernel(h_ref, g_ref, w_ref, kop_ref, vop_ref, kf_ref, vf_ref, *, keys_on_rows):
    hn = _rms(h_ref[0], g_ref[...]).astype(BF16)
    kv = jnp.dot(hn, w_ref[...], preferred_element_type=F32)
    k, v = kv[:, :TOK_WIDTH], kv[:, TOK_WIDTH:]
    last = kf_ref.shape[1]
    kf_ref[0] = k[k.shape[0] - last:]
    vf_ref[0] = v[v.shape[0] - last:]
    kop_ref[0] = (k if keys_on_rows else k.T).astype(BF16)
    vop_ref[0] = (v.T if keys_on_rows else v).astype(BF16)


def _mem_kv_kernel(m_ref, g_ref, w_ref, k_ref, v_ref, kT_ref, vb_ref):
    mn = _rms(m_ref[0], g_ref[0]).astype(BF16)
    kv = jnp.dot(mn, w_ref[0], preferred_element_type=F32)
    k, v = kv[:, :MEM_WIDTH], kv[:, MEM_WIDTH:]
    k_ref[0, 0] = k.reshape(N_MEM, MEM_HEADS, MEM_HEAD_DIM)
    v_ref[0, 0] = v.reshape(N_MEM, MEM_HEADS, MEM_HEAD_DIM)
    kT_ref[0, 0] = k.T.astype(BF16)
    vb_ref[0, 0] = v.astype(BF16)


def _const_spec(shape):
    nd = len(shape)
    return pl.BlockSpec(shape, lambda b, i: (0,) * nd, pipeline_mode=pl.Buffered(1))


def _layer_spec(stacked, l):
    nd = stacked.ndim
    return pl.BlockSpec((None,) + stacked.shape[1:], lambda b, i: (l,) + (0,) * (nd - 1),
                        pipeline_mode=pl.Buffered(1))


def _mem_specs(l):
    return [pl.BlockSpec((None, 1, MEM_WIDTH, N_MEM), lambda b, i: (l, b, 0, 0)),
            pl.BlockSpec((None, 1, N_MEM, MEM_WIDTH), lambda b, i: (l, b, 0, 0))]


def _params():
    return pltpu.CompilerParams(dimension_semantics=("arbitrary", "arbitrary"),
                                vmem_limit_bytes=VMEM_LIMIT)


def _mixer_a(h, tile, l, wts, mk, mv, emit_v):
    nb, seq, _ = h.shape
    chunk = min(A_CHUNK, tile)
    act_spec = pl.BlockSpec((1, tile, D_MODEL), lambda b, i: (b, i, 0))
    out_shape = [jax.ShapeDtypeStruct(h.shape, F32)]
    out_specs = [act_spec]
    if emit_v:
        out_shape.append(jax.ShapeDtypeStruct((nb, seq, TOK_WIDTH), F32))
        out_specs.append(pl.BlockSpec((1, tile, TOK_WIDTH), lambda b, i: (b, i, 0)))
    stacked = [wts[k] for k in ("g_mix", "w_in_a", "g_v", "w_s", "bs_full")]
    res = pl.pallas_call(
        functools.partial(_mixer_a_kernel, chunk=chunk, emit_v=emit_v),
        grid=(nb, seq // tile),
        in_specs=[act_spec] + [_layer_spec(w, l) for w in stacked] + _mem_specs(l)
        + [_layer_spec(wts["w_out"], l)],
        out_specs=out_specs, out_shape=out_shape,
        scratch_shapes=[pltpu.VMEM((tile, 2 * TOK_WIDTH + MEM_WIDTH), F32)]
        + [pltpu.VMEM((tile, N_MEM), F32)] * MEM_HEADS + [pltpu.VMEM((tile, D_MODEL), BF16)],
        compiler_params=_params(), name="mixer_a",
    )(h, *stacked, mk, mv, wts["w_out"])
    return res if emit_v else (res[0], None)


def _mixer_b(h, tile, l, wts, kT_a, kT_b, v_a, v_b, bias, blocks, mk, mv):
    nb, seq, _ = h.shape
    j = l - N_A_LAYERS
    act_spec = pl.BlockSpec((1, tile, D_MODEL), lambda b, i: (b, i, 0))
    ka, kb, va, vb = (pl.BlockSpec((1,) + a.shape[1:], lambda b, i: (b, 0, 0)) for a in (kT_a, kT_b, v_a, v_b))
    return pl.pallas_call(
        functools.partial(_mixer_b_kernel, blocks=blocks),
        grid=(nb, seq // tile),
        in_specs=[act_spec, _layer_spec(wts["g_mix"], l), _layer_spec(wts["w_in_b"], j), ka, kb, va, vb,
                  _layer_spec(bias, j)] + _mem_specs(l) + [_layer_spec(wts["w_out"], l)],
        out_specs=act_spec, out_shape=jax.ShapeDtypeStruct(h.shape, F32),
        scratch_shapes=[pltpu.VMEM((tile, TOK_WIDTH + MEM_WIDTH), F32)]
        + [pltpu.VMEM(bias.shape[3:], F32)] * ATT_SLOTS + [pltpu.VMEM((bias.shape[3], LANES), F32)] * ATT_SLOTS
        + [pltpu.VMEM((tile, N_MEM), F32)] * ATT_SLOTS + [pltpu.VMEM((tile, D_MODEL), BF16)],
        compiler_params=_params(), name="mixer_b",
    )(h, wts["g_mix"], wts["w_in_b"], kT_a, kT_b, v_a, v_b, bias, mk, mv, wts["w_out"])


def _mixer_bt(h, tile, l, wts, k_rows, vT, blocks, mk, mv):
    nb, seq, _ = h.shape
    j = l - N_A_LAYERS
    biasT = wts["bias_prompt_t"]
    kb_keys, qb = biasT.shape[3:]
    act_spec = pl.BlockSpec((1, tile, D_MODEL), lambda b, i: (b, i, 0))
    ka = pl.BlockSpec((1, tile, TOK_WIDTH), lambda b, i: (b, jnp.maximum(i - 1, 0), 0))
    kb = pl.BlockSpec((1, tile, TOK_WIDTH), lambda b, i: (b, i, 0))
    va = pl.BlockSpec((1, TOK_WIDTH, tile), lambda b, i: (b, 0, jnp.maximum(i - 1, 0)))
    vb = pl.BlockSpec((1, TOK_WIDTH, tile), lambda b, i: (b, 0, i))
    return pl.pallas_call(
        functools.partial(_mixer_bt_kernel, blocks=blocks),
        grid=(nb, seq // tile),
        in_specs=[act_spec, _layer_spec(wts["g_mix"], l), _layer_spec(wts["w_q_t"], j),
                  _layer_spec(wts["w_in_b_mem"], j), ka, kb, va, vb, _layer_spec(biasT, j)]
        + _mem_specs(l) + [_layer_spec(wts["w_out"], l)],
        out_specs=act_spec, out_shape=jax.ShapeDtypeStruct(h.shape, F32),
        scratch_shapes=[pltpu.VMEM((TOK_WIDTH, tile), F32), pltpu.VMEM((tile, MEM_WIDTH), F32)]
        + [pltpu.VMEM((kb_keys, qb), F32)] * ATT_SLOTS + [pltpu.VMEM((SUBLANES, qb), F32)] * ATT_SLOTS
        + [pltpu.VMEM((tile, N_MEM), F32)] * ATT_SLOTS + [pltpu.VMEM((tile, D_MODEL), BF16)],
        compiler_params=_params(), name="mixer_b",
    )(h, wts["g_mix"], wts["w_q_t"], wts["w_in_b_mem"], k_rows, k_rows, vT, vT, biasT, mk, mv, wts["w_out"])


def _ffn(h, n_seq, tile, l, wts, prev, final_norm):
    nb, seq, _ = h.shape
    act_spec = pl.BlockSpec((n_seq, tile, D_MODEL), lambda b, i: (b, i, 0))
    stacked = [wts[k] for k in ("g_ffn", "w_up", "conv_w", "conv_b", "w_down")]
    return pl.pallas_call(
        functools.partial(_ffn_kernel, final_norm=final_norm),
        grid=(nb // n_seq, seq // tile),
        in_specs=[act_spec] + [_layer_spec(w, l) for w in stacked]
        + [pl.BlockSpec((None, n_seq, CONV_W - 1, D_FF), lambda b, i: (l, b, 0, 0)),
           _const_spec(wts["g_final"].shape)],
        out_specs=[act_spec, pl.BlockSpec((n_seq, CONV_W - 1, D_FF), lambda b, i: (b, 0, 0))],
        out_shape=[jax.ShapeDtypeStruct(h.shape, F32),
                   jax.ShapeDtypeStruct((nb, CONV_W - 1, D_FF), F32)],
        scratch_shapes=[pltpu.VMEM((n_seq, CONV_W - 1, D_FF), F32),
                        pltpu.VMEM((n_seq * tile, D_MODEL), BF16),
                        pltpu.VMEM((FF_SLOTS, n_seq * tile, FF_CHUNK), F32),
                        pltpu.VMEM((FF_SLOTS, n_seq * tile, FF_CHUNK), F32),
                        pltpu.VMEM((n_seq * tile, D_FF), BF16)],
        compiler_params=_params(), name="conv_ffn",
    )(h, *stacked, prev, wts["g_final"])


def _shared_kv(h, tile, g, w, keys_on_rows):
    nb, seq, _ = h.shape
    n_t = seq // tile
    last = min(tile, B_WINDOW)
    act_spec = pl.BlockSpec((1, tile, D_MODEL), lambda b, i: (b, i, 0))
    last_spec = pl.BlockSpec((1, last, TOK_WIDTH), lambda b, i: (b, 0, 0))
    rows_spec = pl.BlockSpec((1, tile, TOK_WIDTH), lambda b, i: (b, i, 0))
    cols_spec = pl.BlockSpec((1, TOK_WIDTH, tile), lambda b, i: (b, 0, i))
    rows_shape = jax.ShapeDtypeStruct((nb, seq, TOK_WIDTH), BF16)
    cols_shape = jax.ShapeDtypeStruct((nb, TOK_WIDTH, seq), BF16)
    return pl.pallas_call(
        functools.partial(_kv_kernel, keys_on_rows=keys_on_rows),
        grid=(nb, n_t),
        in_specs=[act_spec, _const_spec(g.shape), _const_spec(w.shape)],
        out_specs=[rows_spec, cols_spec, last_spec, last_spec] if keys_on_rows
        else [cols_spec, rows_spec, last_spec, last_spec],
        out_shape=[rows_shape if keys_on_rows else cols_shape,
                   cols_shape if keys_on_rows else rows_shape,
                   jax.ShapeDtypeStruct((nb, last, TOK_WIDTH), F32),
                   jax.ShapeDtypeStruct((nb, last, TOK_WIDTH), F32)],
        compiler_params=_params(), name="shared_kv",
    )(h, g, w)


def _mem_kv(mem, g, w):
    nb = mem.shape[0]
    out_spec = pl.BlockSpec((1, 1, N_MEM, MEM_HEADS, MEM_HEAD_DIM), lambda l, b: (l, b, 0, 0, 0))
    out_shape = jax.ShapeDtypeStruct((DEPTH, nb, N_MEM, MEM_HEADS, MEM_HEAD_DIM), F32)
    return pl.pallas_call(
        _mem_kv_kernel,
        grid=(DEPTH, nb),
        in_specs=[pl.BlockSpec((1, N_MEM, D_MODEL), lambda l, b: (b, 0, 0)),
                  pl.BlockSpec((1, 1, D_MODEL), lambda l, b: (l, 0, 0)),
                  pl.BlockSpec((1, D_MODEL, 2 * MEM_WIDTH), lambda l, b: (l, 0, 0))],
        out_specs=[out_spec, out_spec,
                   pl.BlockSpec((1, 1, MEM_WIDTH, N_MEM), lambda l, b: (l, b, 0, 0)),
                   pl.BlockSpec((1, 1, N_MEM, MEM_WIDTH), lambda l, b: (l, b, 0, 0))],
        out_shape=[out_shape, out_shape,
                   jax.ShapeDtypeStruct((DEPTH, nb, MEM_WIDTH, N_MEM), BF16),
                   jax.ShapeDtypeStruct((DEPTH, nb, N_MEM, MEM_WIDTH), BF16)],
        compiler_params=_params(), name="mem_kv",
    )(mem, g, w)


def _band_bias(rel_bias, q_chunks, k_chunks, first_tile_prev):
    nq, nk = q_chunks * CHUNK, k_chunks * CHUNK
    period = nq + nk
    d = np.arange(period)
    d = np.where(d < nk, d, d - period)
    idx = np.clip(B_WINDOW - d, -REL_CLIP, REL_CLIP) + REL_CLIP
    e = rel_bias.astype(F32)[..., idx]
    lead = e.shape[:-1]
    flat = jnp.tile(e, (1,) * len(lead) + (nq,))[..., :nq * (period - 1)]
    toep = flat.reshape(lead + (nq, period - 1))[..., :nk]
    dc = np.arange(nk)[None, :] // CHUNK - np.arange(nq)[:, None] // CHUNK
    valid = (dc >= 0) & (dc <= B_LEFT_CHUNKS)
    variants = [valid] + [valid & (np.arange(nk)[None, :] >= n_prev) for n_prev in first_tile_prev]
    return jnp.stack([jnp.where(m, toep, NEG_INF) for m in variants], axis=-4)


def _trunk(h, tile, prompt, mk, mv, conv_prev, cache_kT, cache_v, wts):
    a_v, states = [], []
    k_last = v_last = None
    ffn_tile = FFN_TILE if prompt else tile
    ffn_seqs = 1 if prompt else FFN_SAMPLE_ROWS // tile
    wide_tile = WIDE_TILE if prompt else tile
    for l in range(DEPTH):
        if l < N_A_LAYERS:
            h, v_rows = _mixer_a(h, wide_tile, l, wts, mk, mv, emit_v=not prompt)
            if not prompt:
                a_v.append(v_rows)
        else:
            if l == N_A_LAYERS:
                k_op, v_op, k_last, v_last = _shared_kv(h, wide_tile, wts["g_kv"], wts["w_kv"], prompt)
            if prompt:
                blocks = tuple((qb * Q_BLOCK, qb * Q_BLOCK, 1 + qb) for qb in range(tile // Q_BLOCK))
                h = _mixer_bt(h, tile, l, wts, k_op, v_op, blocks, mk, mv)
            else:
                h = _mixer_b(h, tile, l, wts, cache_kT, k_op, cache_v, v_op, wts["bias_sample"], ((0, 0, 0),),
                             mk, mv)
        h, st = _ffn(h, ffn_seqs, ffn_tile, l, wts, conv_prev, final_norm=(l == DEPTH - 1))
        states.append(st)
    return h, a_v, k_last, v_last, jnp.stack(states)


def kernel(x_prompt, x_sample, mem_prompt, cache_mem_k, cache_mem_v, cache_b_k, cache_b_v, state_conv,
           g_mix, w_in_a, g_v, w_s, b_s, g_kv, w_kv, w_in_b, rel_bias, g_mem, w_mem_kv,
           w_out, g_ffn, w_up, conv_w, conv_b, w_down, g_final):
    bsz = x_prompt.shape[0]
    dbs = x_sample.shape[0]
    dec_seq = x_sample.shape[1]
    wts = {
        "g_mix": g_mix.reshape(DEPTH, 1, D_MODEL),
        "w_in_a": w_in_a.astype(BF16),
        "g_v": g_v.reshape(N_A_LAYERS, 1, TOK_WIDTH),
        "w_s": w_s,
        "bs_full": jnp.repeat(b_s.transpose(0, 2, 1), TOK_WIDTH // A_GROUPS, axis=-1),
        "g_kv": g_kv.reshape(1, D_MODEL),
        "w_kv": w_kv.astype(BF16),
        "w_in_b": w_in_b.astype(BF16),
        "w_q_t": w_in_b[:, :, :TOK_WIDTH].transpose(0, 2, 1).astype(BF16),
        "w_in_b_mem": w_in_b[:, :, TOK_WIDTH:].astype(BF16),
        "bias_prompt_t": jnp.swapaxes(
            _band_bias(rel_bias, Q_BLOCK // CHUNK, Q_BLOCK // CHUNK + B_LEFT_CHUNKS,
                       tuple(B_WINDOW - qb * Q_BLOCK for qb in range(PROMPT_TILE // Q_BLOCK))), -1, -2),
        "bias_sample": _band_bias(rel_bias, 1, 1 + B_LEFT_CHUNKS, ()),
        "w_out": w_out.astype(BF16),
        "g_ffn": g_ffn.reshape(DEPTH, 1, D_MODEL),
        "w_up": w_up.astype(BF16),
        "conv_w": conv_w,
        "conv_b": conv_b.reshape(DEPTH, 1, D_FF),
        "w_down": w_down.astype(BF16),
        "g_final": g_final.reshape(1, D_MODEL),
    }

    mem_k_p, mem_v_p, mkT_p, mv_p = _mem_kv(mem_prompt, g_mem.reshape(DEPTH, 1, D_MODEL),
                                            w_mem_kv.astype(BF16))
    conv0 = jnp.zeros((DEPTH, bsz, CONV_W - 1, D_FF), F32)
    y_p, _, k_p, v_p, conv_p = _trunk(x_prompt, PROMPT_TILE, True, mkT_p, mv_p, conv0, None, None, wts)

    mkT_s = cache_mem_k.reshape(DEPTH, dbs, N_MEM, MEM_WIDTH).transpose(0, 1, 3, 2).astype(BF16)
    mv_s = cache_mem_v.reshape(DEPTH, dbs, N_MEM, MEM_WIDTH).astype(BF16)
    cache_kT = cache_b_k.reshape(dbs, -1, TOK_WIDTH).transpose(0, 2, 1).astype(BF16)
    cache_v = cache_b_v.reshape(dbs, -1, TOK_WIDTH).astype(BF16)
    y_s, a_v_s, k_s, v_s, conv_s = _trunk(x_sample, dec_seq, False, mkT_s, mv_s, state_conv,
                                          cache_kT, cache_v, wts)

    hd = (B_HEADS, B_HEAD_DIM)
    return (y_p, y_s,
            mem_k_p, mem_v_p,
            k_p.reshape(bsz, B_WINDOW, *hd), v_p.reshape(bsz, B_WINDOW, *hd),
            conv_p,
            jnp.stack(a_v_s),
            k_s.reshape(dbs, dec_seq, *hd), v_s.reshape(dbs, dec_seq, *hd),
            conv_s)
```

```python
import functools

import numpy as np
import jax
import jax.numpy as jnp
from jax import lax
from jax.experimental import pallas as pl
from jax.experimental.pallas import tpu as pltpu

D_MODEL = 1024
DEPTH = 4
N_A_LAYERS = 2
TOK_WIDTH = 512
MEM_WIDTH = 512
A_CHUNK = 128
A_GROUPS = 8
B_HEADS = 8
B_HEAD_DIM = 64
CHUNK = 64
B_LEFT_CHUNKS = 8
B_WINDOW = B_LEFT_CHUNKS * CHUNK
REL_CLIP = 256
N_MEM = 256
MEM_HEADS = 4
MEM_HEAD_DIM = 128
D_FF = 2816
CONV_W = 3
EPS = 1e-6
NEG_INF = -1e30

LANES = 128
SUBLANES = 8
PROMPT_TILE = 512
FFN_TILE = 1024
WIDE_TILE = 1024
MEM_KV_SEQS = 4
FFN_SAMPLE_ROWS = 512
FF_CHUNK = 256
N_FF_CHUNKS = D_FF // FF_CHUNK
FF_SLOTS = 3
ATT_AHEAD = 3
ATT_SLOTS = ATT_AHEAD + 3
FF_SPLIT = 6 * FF_CHUNK
Q_BLOCK = 256
VMEM_LIMIT = 56 * 1024 * 1024

F32 = jnp.float32
BF16 = jnp.bfloat16


def _rms(x, g):
    ms = jnp.mean(x * x, axis=-1, keepdims=True)
    return x * lax.rsqrt(ms + EPS) * g


def _softmax_parts(s):
    m = jnp.max(s, axis=-1, keepdims=True)
    p = jnp.exp(s - m)
    return p, jnp.sum(p, axis=-1, keepdims=True)


def _run_skewed(stages, ahead):
    n = len(stages)
    for t in range(n + ahead):
        if t < n:
            stages[t][0]()
        if t >= ahead:
            stages[t - ahead][1]()


def _mem_stages(z_ref, col0, mkT_ref, mv_ref, s_refs, cat_ref, first_index):
    stages = []
    for hh in range(MEM_HEADS):
        sl = slice(hh * MEM_HEAD_DIM, (hh + 1) * MEM_HEAD_DIM)
        s_ref = s_refs[(first_index + hh) % len(s_refs)]

        def score(sl=sl, s_ref=s_ref):
            q = (z_ref[:, col0 + sl.start:col0 + sl.stop] * (MEM_HEAD_DIM ** -0.5)).astype(BF16)
            s_ref[...] = jnp.dot(q, mkT_ref[0, sl, :], preferred_element_type=F32)

        def finish(sl=sl, s_ref=s_ref):
            p, l = _softmax_parts(s_ref[...])
            o = jnp.dot(p.astype(BF16), mv_ref[0, :, sl], preferred_element_type=F32)
            cat_ref[:, TOK_WIDTH + sl.start:TOK_WIDTH + sl.stop] = (o * (1.0 / l)).astype(BF16)

        stages.append((score, finish))
    return stages


def _spatial_gate(u, v, ws_ref, bs_ref, chunk):
    rows = u.shape[0]
    r = lax.broadcasted_iota(jnp.int32, (A_CHUNK, A_CHUNK), 0)
    c = lax.broadcasted_iota(jnp.int32, (A_CHUNK, A_CHUNK), 1)
    w_tril = [jnp.where(r >= c, ws_ref[g], 0.0).astype(BF16)[:chunk, :] for g in range(A_GROUPS)]
    w_pair = [jnp.concatenate([w_tril[2 * pr], w_tril[2 * pr + 1]], axis=1) for pr in range(A_GROUPS // 2)]
    n_chunks = rows // chunk
    per_dot = 2 if n_chunks % 2 == 0 else 1
    lane = lax.broadcasted_iota(jnp.int32, (A_CHUNK, per_dot * LANES), 1)
    low = (lane & (LANES // 2)) == 0
    bias = bs_ref[:chunk, :]
    parts = [[None] * (A_GROUPS // 2) for _ in range(n_chunks)]
    for c0 in range(0, n_chunks, per_dot):
        vcs = [v[ci * chunk:(ci + 1) * chunk] for ci in range(c0, c0 + per_dot)]
        if chunk < A_CHUNK:
            vcs = [jnp.concatenate([vc, jnp.zeros((A_CHUNK - chunk, TOK_WIDTH), F32)], axis=0) for vc in vcs]
        for pr in range(A_GROUPS // 2):
            blocks = [vc[:, pr * LANES:(pr + 1) * LANES] for vc in vcs]
            vp = jnp.concatenate(blocks, axis=1) if per_dot > 1 else blocks[0]
            rhs = jnp.concatenate([jnp.where(low, vp, 0.0), jnp.where(low, 0.0, vp)], axis=0).astype(BF16)
            s = jnp.dot(w_pair[pr], rhs, preferred_element_type=F32)
            for k in range(per_dot):
                parts[c0 + k][pr] = s[:, k * LANES:(k + 1) * LANES]
    toks = [u[ci * chunk:(ci + 1) * chunk] * (jnp.concatenate(parts[ci], axis=-1) + bias)
            for ci in range(n_chunks)]
    return jnp.concatenate(toks, axis=0) if len(toks) > 1 else toks[0]


def _mixer_a_kernel(h_ref, gmix_ref, win_ref, gv_ref, ws_ref, bs_ref, mkT_ref, mv_ref, wout_ref,
                    *refs, chunk, emit_v):
    out_ref = refs[0]
    z_ref, cat_ref = refs[-MEM_HEADS - 2], refs[-1]
    ms_refs = refs[-MEM_HEADS - 1:-1]
    x = h_ref[0]
    xn = _rms(x, gmix_ref[...]).astype(BF16)
    z_ref[...] = jnp.dot(xn, win_ref[...], preferred_element_type=F32)
    stages = _mem_stages(z_ref, 2 * TOK_WIDTH, mkT_ref, mv_ref, ms_refs, cat_ref, 0)
    for score, _ in stages:
        score()
    uv = jax.nn.gelu(z_ref[:, :2 * TOK_WIDTH])
    u = uv[:, :TOK_WIDTH]
    v = _rms(uv[:, TOK_WIDTH:], gv_ref[...])
    if emit_v:
        refs[1][0] = v
    cat_ref[:, :TOK_WIDTH] = _spatial_gate(u, v, ws_ref, bs_ref, chunk).astype(BF16)
    for _, finish in stages:
        finish()
    out_ref[0] = x + jnp.dot(cat_ref[...], wout_ref[...], preferred_element_type=F32)


def _window(ref_a, ref_b, axis, other, lo, hi):
    na = ref_a.shape[1 + axis]
    parts = []
    if lo < na:
        sl = slice(lo, min(hi, na))
        parts.append(ref_a[0, sl, other] if axis == 0 else ref_a[0, other, sl])
    if hi > na:
        sl = slice(max(lo - na, 0), hi - na)
        parts.append(ref_b[0, sl, other] if axis == 0 else ref_b[0, other, sl])
    return parts[0] if len(parts) == 1 else jnp.concatenate(parts, axis=axis)


def _band_stages(z_ref, kT_refs, v_refs, bias_ref, blocks, first_tile, s_refs, m_refs, cat_ref):
    qb, kb = bias_ref.shape[2], bias_ref.shape[3]
    lane_q = lax.broadcasted_iota(jnp.int32, (qb, LANES), 1)
    low = lane_q < B_HEAD_DIM
    lane_k = lax.broadcasted_iota(jnp.int32, (kb, LANES), 1)
    low_k = lane_k < B_HEAD_DIM
    stages = []
    held = {}
    for (row0, key0, first_variant) in blocks:
        variant = jnp.where(first_tile, first_variant, 0) if first_variant else 0
        for pr in range(B_HEADS // 2):
            psl = slice(pr * LANES, (pr + 1) * LANES)
            for half in range(2):
                s_ref = s_refs[len(stages) % len(s_refs)]
                m_ref = m_refs[len(stages) % len(m_refs)]

                def score(row0=row0, key0=key0, variant=variant, psl=psl, pr=pr, half=half,
                          s_ref=s_ref, m_ref=m_ref):
                    qp = z_ref[row0:row0 + qb, psl] * (B_HEAD_DIM ** -0.5)
                    qh = (jnp.where(low, qp, 0.0) if half == 0 else jnp.where(low, 0.0, qp)).astype(BF16)
                    s = jnp.dot(qh, _window(*kT_refs, 1, psl, key0, key0 + kb), preferred_element_type=F32)
                    s = s + bias_ref[variant, 2 * pr + half]
                    s_ref[...] = s
                    m = s[:, :LANES]
                    for j in range(1, kb // LANES):
                        m = jnp.maximum(m, s[:, j * LANES:(j + 1) * LANES])
                    m_ref[...] = m

                def finish(row0=row0, key0=key0, psl=psl, half=half, s_ref=s_ref, m_ref=m_ref):
                    m = jnp.max(m_ref[...], axis=-1, keepdims=True)
                    p = jnp.exp(s_ref[...] - m).astype(BF16)
                    vv = _window(*v_refs, 0, psl, key0, key0 + kb)
                    vv = jnp.where(low_k, vv, 1.0) if half == 0 else jnp.where(low_k, 1.0, vv)
                    o = jnp.dot(p, vv.astype(BF16), preferred_element_type=F32)
                    o = o * (1.0 / pltpu.roll(o, B_HEAD_DIM, 1))
                    if half == 0:
                        held[(row0, psl.start)] = o
                    else:
                        pair = jnp.where(low, held.pop((row0, psl.start)), o)
                        cat_ref[row0:row0 + qb, psl] = pair.astype(BF16)

                stages.append((score, finish))
    return stages


def _mixer_b_kernel(h_ref, gmix_ref, win_ref, kTa_ref, kTb_ref, va_ref, vb_ref, bias_ref,
                    mkT_ref, mv_ref, wout_ref, out_ref, z_ref, *scratch, blocks):
    s_refs, m_refs, ms_refs = (scratch[k * ATT_SLOTS:(k + 1) * ATT_SLOTS] for k in range(3))
    cat_ref = scratch[-1]
    x = h_ref[0]
    xn = _rms(x, gmix_ref[...]).astype(BF16)
    z_ref[...] = jnp.dot(xn, win_ref[...], preferred_element_type=F32)
    stages = _band_stages(z_ref, (kTa_ref, kTb_ref), (va_ref, vb_ref), bias_ref, blocks, pl.program_id(1) == 0, s_refs, m_refs, cat_ref)
    stages += _mem_stages(z_ref, TOK_WIDTH, mkT_ref, mv_ref, ms_refs, cat_ref, len(stages))
    _run_skewed(stages, ATT_AHEAD)
    out_ref[0] = x + jnp.dot(cat_ref[...], wout_ref[...], preferred_element_type=F32)


def _band_stages_t(qT_ref, k_refs, vT_refs, biasT_ref, blocks, first_tile, s_refs, m_refs, cat_ref):
    kb, qb = biasT_ref.shape[2], biasT_ref.shape[3]
    top_v = lax.broadcasted_iota(jnp.int32, (LANES, kb), 0) < B_HEAD_DIM
    zeros = jnp.zeros((B_HEAD_DIM, qb), F32)
    stages = []
    held = {}
    for (row0, key0, first_variant) in blocks:
        variant = jnp.where(first_tile, first_variant, 0) if first_variant else 0
        for pr in range(B_HEADS // 2):
            psl = slice(pr * LANES, (pr + 1) * LANES)
            for half in range(2):
                s_ref = s_refs[len(stages) % len(s_refs)]
                m_ref = m_refs[len(stages) % len(m_refs)]

                def score(row0=row0, key0=key0, variant=variant, psl=psl, pr=pr, half=half,
                          s_ref=s_ref, m_ref=m_ref):
                    qp = qT_ref[psl, row0:row0 + qb]
                    qh = jnp.concatenate([qp[:B_HEAD_DIM], zeros] if half == 0 else [zeros, qp[B_HEAD_DIM:]], axis=0)
                    s = jnp.dot(_window(*k_refs, 0, psl, key0, key0 + kb), qh.astype(BF16),
                                preferred_element_type=F32)
                    s = s + biasT_ref[variant, 2 * pr + half]
                    s_ref[...] = s
                    m_ref[...] = jnp.max(s.reshape(kb // SUBLANES, SUBLANES, qb), axis=0)

                def finish(row0=row0, key0=key0, psl=psl, half=half, s_ref=s_ref, m_ref=m_ref):
                    m = jnp.max(m_ref[...], axis=0, keepdims=True)
                    p = jnp.exp(s_ref[...] - m).astype(BF16)
                    vT = _window(*vT_refs, 1, psl, key0, key0 + kb)
                    vT = jnp.where(top_v, vT, 1.0) if half == 0 else jnp.where(top_v, 1.0, vT)
                    o = jnp.dot(vT.astype(BF16), p, preferred_element_type=F32)
                    if half == 0:
                        held[(row0, psl.start)] = o[:B_HEAD_DIM] * (1.0 / o[B_HEAD_DIM:B_HEAD_DIM + 1])
                    else:
                        on = o[B_HEAD_DIM:] * (1.0 / o[0:1])
                        pair_t = jnp.concatenate([held.pop((row0, psl.start)), on], axis=0)
                        cat_ref[row0:row0 + qb, psl] = pair_t.T.astype(BF16)

                stages.append((score, finish))
    return stages


def _mixer_bt_kernel(h_ref, gmix_ref, wqT_ref, wmem_ref, ka_ref, kb_ref, vTa_ref, vTb_ref, biasT_ref,
                     mkT_ref, mv_ref, wout_ref, out_ref, qT_ref, z_ref, *scratch, blocks):
    s_refs, m_refs, ms_refs = (scratch[k * ATT_SLOTS:(k + 1) * ATT_SLOTS] for k in range(3))
    cat_ref = scratch[-1]
    x = h_ref[0]
    xn = _rms(x, gmix_ref[...]).astype(BF16)
    qT = lax.dot_general(wqT_ref[...], xn, (((1,), (1,)), ((), ())), preferred_element_type=F32)
    qT_ref[...] = qT * (B_HEAD_DIM ** -0.5)
    z_ref[...] = jnp.dot(xn, wmem_ref[...], preferred_element_type=F32)
    stages = _band_stages_t(qT_ref, (ka_ref, kb_ref), (vTa_ref, vTb_ref), biasT_ref, blocks,
                            pl.program_id(1) == 0, s_refs, m_refs, cat_ref)
    stages += _mem_stages(z_ref, 0, mkT_ref, mv_ref, ms_refs, cat_ref, len(stages))
    _run_skewed(stages, ATT_AHEAD)
    out_ref[0] = x + jnp.dot(cat_ref[...], wout_ref[...], preferred_element_type=F32)


def _mixer_bt_kv_kernel(h_ref, gmix_ref, wqT_ref, wmem_ref, gkv_ref, wkv_ref, biasT_ref, mkT_ref, mv_ref,
                        wout_ref, out_ref, kop_ref, vop_ref, kf_ref, vf_ref,
                        qT_ref, z_ref, kprev_ref, kcur_ref, vprev_ref, vcur_ref, *scratch, blocks):
    s_refs, m_refs, ms_refs = (scratch[k * ATT_SLOTS:(k + 1) * ATT_SLOTS] for k in range(3))
    cat_ref = scratch[-1]

    @pl.when(pl.program_id(1) == 0)
    def _():
        kprev_ref[...] = jnp.zeros_like(kprev_ref)
        vprev_ref[...] = jnp.zeros_like(vprev_ref)

    @pl.when(pl.program_id(1) > 0)
    def _():
        kprev_ref[...] = kcur_ref[...]
        vprev_ref[...] = vcur_ref[...]

    x = h_ref[0]
    kv = jnp.dot(_rms(x, gkv_ref[...]).astype(BF16), wkv_ref[...], preferred_element_type=F32)
    k, v = kv[:, :TOK_WIDTH], kv[:, TOK_WIDTH:]
    kf_ref[0] = k
    vf_ref[0] = v
    kcur_ref[0] = k.astype(BF16)
    vcur_ref[0] = v.T.astype(BF16)
    xn = _rms(x, gmix_ref[...]).astype(BF16)
    qT = lax.dot_general(wqT_ref[...], xn, (((1,), (1,)), ((), ())), preferred_element_type=F32)
    qT_ref[...] = qT * (B_HEAD_DIM ** -0.5)
    z_ref[...] = jnp.dot(xn, wmem_ref[...], preferred_element_type=F32)
    stages = _band_stages_t(qT_ref, (kprev_ref, kcur_ref), (vprev_ref, vcur_ref), biasT_ref, blocks,
                            pl.program_id(1) == 0, s_refs, m_refs, cat_ref)
    stages += _mem_stages(z_ref, 0, mkT_ref, mv_ref, ms_refs, cat_ref, len(stages))
    _run_skewed(stages, ATT_AHEAD)
    out_ref[0] = x + jnp.dot(cat_ref[...], wout_ref[...], preferred_element_type=F32)
    kop_ref[...] = kcur_ref[...]
    vop_ref[...] = vcur_ref[...]


def _ffn_kernel(h_ref, g_ref, wup_ref, cw_ref, cb_ref, wd_ref, prev_ref, gfin_ref,
                out_ref, state_ref, carry_ref, xn_ref, a_ref, gate_ref, act_ref, *, final_norm):
    @pl.when(pl.program_id(1) == 0)
    def _():
        carry_ref[...] = prev_ref[...]

    n_seq, seq_len, _ = h_ref.shape
    rows = n_seq * seq_len
    x = h_ref[...].reshape(rows, D_MODEL)
    xn_ref[...] = _rms(x, g_ref[...]).astype(BF16)
    head_row = lax.broadcasted_iota(jnp.int32, (SUBLANES, FF_CHUNK), 0)

    for f in range(N_FF_CHUNKS):
        slot = f % FF_SLOTS
        fsl = slice(f * FF_CHUNK, (f + 1) * FF_CHUNK)
        a_ref[slot] = jnp.dot(xn_ref[...], wup_ref[:, fsl], preferred_element_type=F32)
        gate_ref[slot] = jnp.dot(xn_ref[...], wup_ref[:, D_FF + f * FF_CHUNK:D_FF + (f + 1) * FF_CHUNK],
                                 preferred_element_type=F32)
        a = a_ref[slot]
        gate = gate_ref[slot]
        w0, w1, w2 = cw_ref[0:1, fsl], cw_ref[1:2, fsl], cw_ref[2:3, fsl]
        ys = []
        for s in range(n_seq):
            a_s = a[s * seq_len:(s + 1) * seq_len]
            p0, p1 = carry_ref[s, 0:1, fsl], carry_ref[s, 1:2, fsl]
            r1, r2 = pltpu.roll(a_s, 1, 0), pltpu.roll(a_s, 2, 0)
            a1 = jnp.concatenate([jnp.where(head_row == 0, p1, r1[:SUBLANES]), r1[SUBLANES:]], axis=0)
            a2 = jnp.concatenate([jnp.where(head_row == 0, p0, jnp.where(head_row == 1, p1, r2[:SUBLANES])),
                                  r2[SUBLANES:]], axis=0)
            carry_ref[s, :, fsl] = a_s[seq_len - 2:seq_len]
            ys.append(cb_ref[:, fsl] + w0 * a2 + w1 * a1 + w2 * a_s)
        y = jnp.concatenate(ys, axis=0) if n_seq > 1 else ys[0]
        act_ref[:, fsl] = (jax.nn.gelu(y) * gate).astype(BF16)

    state_ref[...] = carry_ref[...]
    hn = x
    for lo, hi in ((0, FF_SPLIT), (FF_SPLIT, D_FF)):
        hn = hn + jnp.dot(act_ref[:, lo:hi], wd_ref[lo:hi, :], preferred_element_type=F32)
    out = _rms(hn, gfin_ref[...]) if final_norm else hn
    out_ref[...] = out.reshape(n_seq, seq_len, D_MODEL)


def _kv_kernel(h_ref, g_ref, w_ref, kT_ref, v_ref, kf_ref, vf_ref):
    hn = _rms(h_ref[0], g_ref[...]).astype(BF16)
    kv = jnp.dot(hn, w_ref[...], preferred_element_type=F32)
    k, v = kv[:, :TOK_WIDTH], kv[:, TOK_WIDTH:]
    kf_ref[0] = k
    vf_ref[0] = v
    kT_ref[0] = k.T.astype(BF16)
    v_ref[0] = v.astype(BF16)


def _mem_kv_kernel(m_ref, g_ref, w_ref, k_ref, v_ref, kT_ref, vb_ref):
    n = m_ref.shape[0]
    mn = _rms(m_ref[...].reshape(n * N_MEM, D_MODEL), g_ref[0]).astype(BF16)
    kv = jnp.dot(mn, w_ref[0], preferred_element_type=F32)
    for s in range(n):
        k, v = kv[s * N_MEM:(s + 1) * N_MEM, :MEM_WIDTH], kv[s * N_MEM:(s + 1) * N_MEM, MEM_WIDTH:]
        k_ref[0, s] = k.reshape(N_MEM, MEM_HEADS, MEM_HEAD_DIM)
        v_ref[0, s] = v.reshape(N_MEM, MEM_HEADS, MEM_HEAD_DIM)
        kT_ref[0, s] = k.T.astype(BF16)
        vb_ref[0, s] = v.astype(BF16)


def _const_spec(shape):
    nd = len(shape)
    return pl.BlockSpec(shape, lambda b, i: (0,) * nd, pipeline_mode=pl.Buffered(1))


def _layer_spec(stacked, l):
    nd = stacked.ndim
    return pl.BlockSpec((None,) + stacked.shape[1:], lambda b, i: (l,) + (0,) * (nd - 1),
                        pipeline_mode=pl.Buffered(1))


def _mem_specs(l):
    return [pl.BlockSpec((None, 1, MEM_WIDTH, N_MEM), lambda b, i: (l, b, 0, 0)),
            pl.BlockSpec((None, 1, N_MEM, MEM_WIDTH), lambda b, i: (l, b, 0, 0))]


def _params():
    return pltpu.CompilerParams(dimension_semantics=("arbitrary", "arbitrary"),
                                vmem_limit_bytes=VMEM_LIMIT)


def _mixer_a(h, tile, l, wts, mk, mv, emit_v):
    nb, seq, _ = h.shape
    chunk = min(A_CHUNK, tile)
    act_spec = pl.BlockSpec((1, tile, D_MODEL), lambda b, i: (b, i, 0))
    out_shape = [jax.ShapeDtypeStruct(h.shape, F32)]
    out_specs = [act_spec]
    if emit_v:
        out_shape.append(jax.ShapeDtypeStruct((nb, seq, TOK_WIDTH), F32))
        out_specs.append(pl.BlockSpec((1, tile, TOK_WIDTH), lambda b, i: (b, i, 0)))
    stacked = [wts[k] for k in ("g_mix", "w_in_a", "g_v", "w_s", "bs_full")]
    res = pl.pallas_call(
        functools.partial(_mixer_a_kernel, chunk=chunk, emit_v=emit_v),
        grid=(nb, seq // tile),
        in_specs=[act_spec] + [_layer_spec(w, l) for w in stacked] + _mem_specs(l)
        + [_layer_spec(wts["w_out"], l)],
        out_specs=out_specs, out_shape=out_shape,
        scratch_shapes=[pltpu.VMEM((tile, 2 * TOK_WIDTH + MEM_WIDTH), F32)]
        + [pltpu.VMEM((tile, N_MEM), F32)] * MEM_HEADS + [pltpu.VMEM((tile, D_MODEL), BF16)],
        compiler_params=_params(), name="mixer_a",
    )(h, *stacked, mk, mv, wts["w_out"])
    return res if emit_v else (res[0], None)


def _mixer_b(h, tile, l, wts, kT_a, kT_b, v_a, v_b, bias, blocks, mk, mv):
    nb, seq, _ = h.shape
    j = l - N_A_LAYERS
    act_spec = pl.BlockSpec((1, tile, D_MODEL), lambda b, i: (b, i, 0))
    ka, kb, va, vb = (pl.BlockSpec((1,) + a.shape[1:], lambda b, i: (b, 0, 0)) for a in (kT_a, kT_b, v_a, v_b))
    return pl.pallas_call(
        functools.partial(_mixer_b_kernel, blocks=blocks),
        grid=(nb, seq // tile),
        in_specs=[act_spec, _layer_spec(wts["g_mix"], l), _layer_spec(wts["w_in_b"], j), ka, kb, va, vb,
                  _layer_spec(bias, j)] + _mem_specs(l) + [_layer_spec(wts["w_out"], l)],
        out_specs=act_spec, out_shape=jax.ShapeDtypeStruct(h.shape, F32),
        scratch_shapes=[pltpu.VMEM((tile, TOK_WIDTH + MEM_WIDTH), F32)]
        + [pltpu.VMEM(bias.shape[3:], F32)] * ATT_SLOTS + [pltpu.VMEM((bias.shape[3], LANES), F32)] * ATT_SLOTS
        + [pltpu.VMEM((tile, N_MEM), F32)] * ATT_SLOTS + [pltpu.VMEM((tile, D_MODEL), BF16)],
        compiler_params=_params(), name="mixer_b",
    )(h, wts["g_mix"], wts["w_in_b"], kT_a, kT_b, v_a, v_b, bias, mk, mv, wts["w_out"])


def _mixer_bt(h, tile, l, wts, k_rows, vT, blocks, mk, mv):
    nb, seq, _ = h.shape
    j = l - N_A_LAYERS
    biasT = wts["bias_prompt_t"]
    kb_keys, qb = biasT.shape[3:]
    act_spec = pl.BlockSpec((1, tile, D_MODEL), lambda b, i: (b, i, 0))
    ka = pl.BlockSpec((1, tile, TOK_WIDTH), lambda b, i: (b, jnp.maximum(i - 1, 0), 0))
    kb = pl.BlockSpec((1, tile, TOK_WIDTH), lambda b, i: (b, i, 0))
    va = pl.BlockSpec((1, TOK_WIDTH, tile), lambda b, i: (b, 0, jnp.maximum(i - 1, 0)))
    vb = pl.BlockSpec((1, TOK_WIDTH, tile), lambda b, i: (b, 0, i))
    return pl.pallas_call(
        functools.partial(_mixer_bt_kernel, blocks=blocks),
        grid=(nb, seq // tile),
        in_specs=[act_spec, _layer_spec(wts["g_mix"], l), _layer_spec(wts["w_q_t"], j),
                  _layer_spec(wts["w_in_b_mem"], j), ka, kb, va, vb, _layer_spec(biasT, j)]
        + _mem_specs(l) + [_layer_spec(wts["w_out"], l)],
        out_specs=act_spec, out_shape=jax.ShapeDtypeStruct(h.shape, F32),
        scratch_shapes=[pltpu.VMEM((TOK_WIDTH, tile), F32), pltpu.VMEM((tile, MEM_WIDTH), F32)]
        + [pltpu.VMEM((kb_keys, qb), F32)] * ATT_SLOTS + [pltpu.VMEM((SUBLANES, qb), F32)] * ATT_SLOTS
        + [pltpu.VMEM((tile, N_MEM), F32)] * ATT_SLOTS + [pltpu.VMEM((tile, D_MODEL), BF16)],
        compiler_params=_params(), name="mixer_b",
    )(h, wts["g_mix"], wts["w_q_t"], wts["w_in_b_mem"], k_rows, k_rows, vT, vT, biasT, mk, mv, wts["w_out"])


def _mixer_bt_kv(h, tile, l, wts, blocks, mk, mv):
    nb, seq, _ = h.shape
    j = l - N_A_LAYERS
    biasT = wts["bias_prompt_t"]
    kb_keys, qb = biasT.shape[3:]
    act_spec = pl.BlockSpec((1, tile, D_MODEL), lambda b, i: (b, i, 0))
    rows_spec = pl.BlockSpec((1, tile, TOK_WIDTH), lambda b, i: (b, i, 0))
    cols_spec = pl.BlockSpec((1, TOK_WIDTH, tile), lambda b, i: (b, 0, i))
    last_spec = pl.BlockSpec((1, tile, TOK_WIDTH), lambda b, i: (b, 0, 0))
    return pl.pallas_call(
        functools.partial(_mixer_bt_kv_kernel, blocks=blocks),
        grid=(nb, seq // tile),
        in_specs=[act_spec, _layer_spec(wts["g_mix"], l), _layer_spec(wts["w_q_t"], j),
                  _layer_spec(wts["w_in_b_mem"], j), _const_spec(wts["g_kv"].shape), _const_spec(wts["w_kv"].shape),
                  _layer_spec(biasT, j)] + _mem_specs(l) + [_layer_spec(wts["w_out"], l)],
        out_specs=[act_spec, rows_spec, cols_spec, last_spec, last_spec],
        out_shape=[jax.ShapeDtypeStruct(h.shape, F32),
                   jax.ShapeDtypeStruct((nb, seq, TOK_WIDTH), BF16),
                   jax.ShapeDtypeStruct((nb, TOK_WIDTH, seq), BF16),
                   jax.ShapeDtypeStruct((nb, tile, TOK_WIDTH), F32),
                   jax.ShapeDtypeStruct((nb, tile, TOK_WIDTH), F32)],
        scratch_shapes=[pltpu.VMEM((TOK_WIDTH, tile), F32), pltpu.VMEM((tile, MEM_WIDTH), F32),
                        pltpu.VMEM((1, tile, TOK_WIDTH), BF16), pltpu.VMEM((1, tile, TOK_WIDTH), BF16),
                        pltpu.VMEM((1, TOK_WIDTH, tile), BF16), pltpu.VMEM((1, TOK_WIDTH, tile), BF16)]
        + [pltpu.VMEM((kb_keys, qb), F32)] * ATT_SLOTS + [pltpu.VMEM((SUBLANES, qb), F32)] * ATT_SLOTS
        + [pltpu.VMEM((tile, N_MEM), F32)] * ATT_SLOTS + [pltpu.VMEM((tile, D_MODEL), BF16)],
        compiler_params=_params(), name="mixer_b_kv",
    )(h, wts["g_mix"], wts["w_q_t"], wts["w_in_b_mem"], wts["g_kv"], wts["w_kv"], biasT, mk, mv, wts["w_out"])


def _ffn(h, n_seq, tile, l, wts, prev, final_norm):
    nb, seq, _ = h.shape
    act_spec = pl.BlockSpec((n_seq, tile, D_MODEL), lambda b, i: (b, i, 0))
    stacked = [wts[k] for k in ("g_ffn", "w_up", "conv_w", "conv_b", "w_down")]
    return pl.pallas_call(
        functools.partial(_ffn_kernel, final_norm=final_norm),
        grid=(nb // n_seq, seq // tile),
        in_specs=[act_spec] + [_layer_spec(w, l) for w in stacked]
        + [pl.BlockSpec((None, n_seq, CONV_W - 1, D_FF), lambda b, i: (l, b, 0, 0)),
           _const_spec(wts["g_final"].shape)],
        out_specs=[act_spec, pl.BlockSpec((n_seq, CONV_W - 1, D_FF), lambda b, i: (b, 0, 0))],
        out_shape=[jax.ShapeDtypeStruct(h.shape, F32),
                   jax.ShapeDtypeStruct((nb, CONV_W - 1, D_FF), F32)],
        scratch_shapes=[pltpu.VMEM((n_seq, CONV_W - 1, D_FF), F32),
                        pltpu.VMEM((n_seq * tile, D_MODEL), BF16),
                        pltpu.VMEM((FF_SLOTS, n_seq * tile, FF_CHUNK), F32),
                        pltpu.VMEM((FF_SLOTS, n_seq * tile, FF_CHUNK), F32),
                        pltpu.VMEM((n_seq * tile, D_FF), BF16)],
        compiler_params=_params(), name="conv_ffn",
    )(h, *stacked, prev, wts["g_final"])


def _shared_kv(h, tile, g, w):
    nb, seq, _ = h.shape
    act_spec = pl.BlockSpec((1, tile, D_MODEL), lambda b, i: (b, i, 0))
    rows_spec = pl.BlockSpec((1, tile, TOK_WIDTH), lambda b, i: (b, i, 0))
    return pl.pallas_call(
        _kv_kernel,
        grid=(nb, seq // tile),
        in_specs=[act_spec, _const_spec(g.shape), _const_spec(w.shape)],
        out_specs=[pl.BlockSpec((1, TOK_WIDTH, tile), lambda b, i: (b, 0, i)), rows_spec, rows_spec, rows_spec],
        out_shape=[jax.ShapeDtypeStruct((nb, TOK_WIDTH, seq), BF16),
                   jax.ShapeDtypeStruct((nb, seq, TOK_WIDTH), BF16),
                   jax.ShapeDtypeStruct((nb, seq, TOK_WIDTH), F32),
                   jax.ShapeDtypeStruct((nb, seq, TOK_WIDTH), F32)],
        compiler_params=_params(), name="shared_kv",
    )(h, g, w)


def _mem_kv(mem, g, w):
    nb = mem.shape[0]
    n = MEM_KV_SEQS
    out_spec = pl.BlockSpec((1, n, N_MEM, MEM_HEADS, MEM_HEAD_DIM), lambda l, b: (l, b, 0, 0, 0))
    out_shape = jax.ShapeDtypeStruct((DEPTH, nb, N_MEM, MEM_HEADS, MEM_HEAD_DIM), F32)
    return pl.pallas_call(
        _mem_kv_kernel,
        grid=(DEPTH, nb // n),
        in_specs=[pl.BlockSpec((n, N_MEM, D_MODEL), lambda l, b: (b, 0, 0)),
                  pl.BlockSpec((1, 1, D_MODEL), lambda l, b: (l, 0, 0)),
                  pl.BlockSpec((1, D_MODEL, 2 * MEM_WIDTH), lambda l, b: (l, 0, 0))],
        out_specs=[out_spec, out_spec,
                   pl.BlockSpec((1, n, MEM_WIDTH, N_MEM), lambda l, b: (l, b, 0, 0)),
                   pl.BlockSpec((1, n, N_MEM, MEM_WIDTH), lambda l, b: (l, b, 0, 0))],
        out_shape=[out_shape, out_shape,
                   jax.ShapeDtypeStruct((DEPTH, nb, MEM_WIDTH, N_MEM), BF16),
                   jax.ShapeDtypeStruct((DEPTH, nb, N_MEM, MEM_WIDTH), BF16)],
        compiler_params=_params(), name="mem_kv",
    )(mem, g, w)


def _band_bias(rel_bias, q_chunks, k_chunks, first_tile_prev, keys_on_rows):
    nq, nk = q_chunks * CHUNK, k_chunks * CHUNK
    rows, cols = (nk, nq) if keys_on_rows else (nq, nk)
    period = nq + nk
    d = np.arange(period)
    d = np.where(d < cols, d, d - period)
    q_minus_k = d if keys_on_rows else -d
    idx = np.clip(B_WINDOW + q_minus_k, -REL_CLIP, REL_CLIP) + REL_CLIP
    e = rel_bias.astype(F32)[..., idx]
    lead = e.shape[:-1]
    flat = jnp.tile(e, (1,) * len(lead) + (rows,))[..., :rows * (period - 1)]
    toep = flat.reshape(lead + (rows, period - 1))[..., :cols]
    q_pos, k_pos = np.arange(nq)[:, None], np.arange(nk)[None, :]
    dc = k_pos // CHUNK - q_pos // CHUNK
    valid = (dc >= 0) & (dc <= B_LEFT_CHUNKS)
    variants = [valid] + [valid & (k_pos >= n_prev) for n_prev in first_tile_prev]
    if keys_on_rows:
        variants = [m.T for m in variants]
    return jnp.stack([jnp.where(m, toep, NEG_INF) for m in variants], axis=-4)


def _trunk(h, tile, prompt, mk, mv, conv_prev, cache_kT, cache_v, wts):
    a_v, states = [], []
    k_last = v_last = None
    ffn_tile = FFN_TILE if prompt else tile
    ffn_seqs = 1 if prompt else FFN_SAMPLE_ROWS // tile
    wide_tile = WIDE_TILE if prompt else tile
    for l in range(DEPTH):
        if l < N_A_LAYERS:
            h, v_rows = _mixer_a(h, wide_tile, l, wts, mk, mv, emit_v=not prompt)
            if not prompt:
                a_v.append(v_rows)
        else:
            if prompt:
                blocks = tuple((qb * Q_BLOCK, qb * Q_BLOCK, 1 + qb) for qb in range(tile // Q_BLOCK))
                if l == N_A_LAYERS:
                    h, k_op, v_op, k_last, v_last = _mixer_bt_kv(h, tile, l, wts, blocks, mk, mv)
                else:
                    h = _mixer_bt(h, tile, l, wts, k_op, v_op, blocks, mk, mv)
            else:
                if l == N_A_LAYERS:
                    k_op, v_op, k_last, v_last = _shared_kv(h, tile, wts["g_kv"], wts["w_kv"])
                h = _mixer_b(h, tile, l, wts, cache_kT, k_op, cache_v, v_op, wts["bias_sample"], ((0, 0, 0),),
                             mk, mv)
        h, st = _ffn(h, ffn_seqs, ffn_tile, l, wts, conv_prev, final_norm=(l == DEPTH - 1))
        states.append(st)
    return h, a_v, k_last, v_last, jnp.stack(states)


def kernel(x_prompt, x_sample, mem_prompt, cache_mem_k, cache_mem_v, cache_b_k, cache_b_v, state_conv,
           g_mix, w_in_a, g_v, w_s, b_s, g_kv, w_kv, w_in_b, rel_bias, g_mem, w_mem_kv,
           w_out, g_ffn, w_up, conv_w, conv_b, w_down, g_final):
    bsz = x_prompt.shape[0]
    dbs = x_sample.shape[0]
    dec_seq = x_sample.shape[1]
    wts = {
        "g_mix": g_mix.reshape(DEPTH, 1, D_MODEL),
        "w_in_a": w_in_a.astype(BF16),
        "g_v": g_v.reshape(N_A_LAYERS, 1, TOK_WIDTH),
        "w_s": w_s,
        "bs_full": jnp.repeat(b_s.transpose(0, 2, 1), TOK_WIDTH // A_GROUPS, axis=-1),
        "g_kv": g_kv.reshape(1, D_MODEL),
        "w_kv": w_kv.astype(BF16),
        "w_in_b": w_in_b.astype(BF16),
        "w_q_t": w_in_b[:, :, :TOK_WIDTH].transpose(0, 2, 1).astype(BF16),
        "w_in_b_mem": w_in_b[:, :, TOK_WIDTH:].astype(BF16),
        "bias_prompt_t": _band_bias(rel_bias, Q_BLOCK // CHUNK, Q_BLOCK // CHUNK + B_LEFT_CHUNKS,
                                    tuple(B_WINDOW - qb * Q_BLOCK for qb in range(PROMPT_TILE // Q_BLOCK)), True),
        "bias_sample": _band_bias(rel_bias, 1, 1 + B_LEFT_CHUNKS, (), False),
        "w_out": w_out.astype(BF16),
        "g_ffn": g_ffn.reshape(DEPTH, 1, D_MODEL),
        "w_up": w_up.astype(BF16),
        "conv_w": conv_w,
        "conv_b": conv_b.reshape(DEPTH, 1, D_FF),
        "w_down": w_down.astype(BF16),
        "g_final": g_final.reshape(1, D_MODEL),
    }

    mem_k_p, mem_v_p, mkT_p, mv_p = _mem_kv(mem_prompt, g_mem.reshape(DEPTH, 1, D_MODEL),
                                            w_mem_kv.astype(BF16))
    conv0 = jnp.zeros((DEPTH, bsz, CONV_W - 1, D_FF), F32)
    y_p, _, k_p, v_p, conv_p = _trunk(x_prompt, PROMPT_TILE, True, mkT_p, mv_p, conv0, None, None, wts)

    mkT_s = cache_mem_k.reshape(DEPTH, dbs, N_MEM, MEM_WIDTH).transpose(0, 1, 3, 2).astype(BF16)
    mv_s = cache_mem_v.reshape(DEPTH, dbs, N_MEM, MEM_WIDTH).astype(BF16)
    cache_kT = cache_b_k.reshape(dbs, -1, TOK_WIDTH).transpose(0, 2, 1).astype(BF16)
    cache_v = cache_b_v.reshape(dbs, -1, TOK_WIDTH).astype(BF16)
    y_s, a_v_s, k_s, v_s, conv_s = _trunk(x_sample, dec_seq, False, mkT_s, mv_s, state_conv,
                                          cache_kT, cache_v, wts)

    hd = (B_HEADS, B_HEAD_DIM)
    return (y_p, y_s,
            mem_k_p, mem_v_p,
            k_p.reshape(bsz, B_WINDOW, *hd), v_p.reshape(bsz, B_WINDOW, *hd),
            conv_p,
            jnp.stack(a_v_s),
            k_s.reshape(dbs, dec_seq, *hd), v_s.reshape(dbs, dec_seq, *hd),
            conv_s)
```

```python
import functools

import numpy as np
import jax
import jax.numpy as jnp
from jax import lax
from jax.experimental import pallas as pl
from jax.experimental.pallas import tpu as pltpu

D_MODEL = 1024
DEPTH = 4
N_A_LAYERS = 2
TOK_WIDTH = 512
MEM_WIDTH = 512
A_CHUNK = 128
A_GROUPS = 8
B_HEADS = 8
B_HEAD_DIM = 64
CHUNK = 64
B_LEFT_CHUNKS = 8
B_WINDOW = B_LEFT_CHUNKS * CHUNK
REL_CLIP = 256
N_MEM = 256
MEM_HEADS = 4
MEM_HEAD_DIM = 128
D_FF = 2816
CONV_W = 3
EPS = 1e-6
NEG_INF = -1e30

LANES = 128
SUBLANES = 8
PROMPT_TILE = 512
FFN_TILE = 1024
WIDE_TILE = 1024
MEM_KV_SEQS = 4
CHUNK_SHIFT = CHUNK.bit_length() - 1
BIAS_PERIOD = 1024
FFN_SAMPLE_ROWS = 512
FF_CHUNK = 256
N_FF_CHUNKS = D_FF // FF_CHUNK
FF_SLOTS = 3
ATT_AHEAD = 3
ATT_SLOTS = ATT_AHEAD + 3
FF_SPLIT = 6 * FF_CHUNK
Q_BLOCK = 256
VMEM_LIMIT = 56 * 1024 * 1024

F32 = jnp.float32
BF16 = jnp.bfloat16


def _rms(x, g):
    ms = jnp.mean(x * x, axis=-1, keepdims=True)
    return x * lax.rsqrt(ms + EPS) * g


def _softmax_parts(s):
    m = jnp.max(s, axis=-1, keepdims=True)
    p = jnp.exp(s - m)
    return p, jnp.sum(p, axis=-1, keepdims=True)


def _run_skewed(stages, ahead):
    n = len(stages)
    for t in range(n + ahead):
        if t < n:
            stages[t][0]()
        if t >= ahead:
            stages[t - ahead][1]()


def _mem_stages(z_ref, col0, mkT_ref, mv_ref, s_refs, cat_ref, first_index):
    stages = []
    for hh in range(MEM_HEADS):
        sl = slice(hh * MEM_HEAD_DIM, (hh + 1) * MEM_HEAD_DIM)
        s_ref = s_refs[(first_index + hh) % len(s_refs)]

        def score(sl=sl, s_ref=s_ref):
            q = (z_ref[:, col0 + sl.start:col0 + sl.stop] * (MEM_HEAD_DIM ** -0.5)).astype(BF16)
            s_ref[...] = jnp.dot(q, mkT_ref[0, sl, :], preferred_element_type=F32)

        def finish(sl=sl, s_ref=s_ref):
            p, l = _softmax_parts(s_ref[...])
            o = jnp.dot(p.astype(BF16), mv_ref[0, :, sl], preferred_element_type=F32)
            cat_ref[:, TOK_WIDTH + sl.start:TOK_WIDTH + sl.stop] = (o * (1.0 / l)).astype(BF16)

        stages.append((score, finish))
    return stages


def _spatial_gate(u, v, ws_ref, bs_ref, chunk):
    rows = u.shape[0]
    r = lax.broadcasted_iota(jnp.int32, (A_CHUNK, A_CHUNK), 0)
    c = lax.broadcasted_iota(jnp.int32, (A_CHUNK, A_CHUNK), 1)
    w_tril = [jnp.where(r >= c, ws_ref[g], 0.0).astype(BF16)[:chunk, :] for g in range(A_GROUPS)]
    w_pair = [jnp.concatenate([w_tril[2 * pr], w_tril[2 * pr + 1]], axis=1) for pr in range(A_GROUPS // 2)]
    n_chunks = rows // chunk
    per_dot = 2 if n_chunks % 2 == 0 else 1
    lane = lax.broadcasted_iota(jnp.int32, (A_CHUNK, per_dot * LANES), 1)
    low = (lane & (LANES // 2)) == 0
    bias = bs_ref[:chunk, :]
    parts = [[None] * (A_GROUPS // 2) for _ in range(n_chunks)]
    for c0 in range(0, n_chunks, per_dot):
        vcs = [v[ci * chunk:(ci + 1) * chunk] for ci in range(c0, c0 + per_dot)]
        if chunk < A_CHUNK:
            vcs = [jnp.concatenate([vc, jnp.zeros((A_CHUNK - chunk, TOK_WIDTH), F32)], axis=0) for vc in vcs]
        for pr in range(A_GROUPS // 2):
            blocks = [vc[:, pr * LANES:(pr + 1) * LANES] for vc in vcs]
            vp = jnp.concatenate(blocks, axis=1) if per_dot > 1 else blocks[0]
            rhs = jnp.concatenate([jnp.where(low, vp, 0.0), jnp.where(low, 0.0, vp)], axis=0).astype(BF16)
            s = jnp.dot(w_pair[pr], rhs, preferred_element_type=F32)
            for k in range(per_dot):
                parts[c0 + k][pr] = s[:, k * LANES:(k + 1) * LANES]
    toks = [u[ci * chunk:(ci + 1) * chunk] * (jnp.concatenate(parts[ci], axis=-1) + bias)
            for ci in range(n_chunks)]
    return jnp.concatenate(toks, axis=0) if len(toks) > 1 else toks[0]


def _mixer_a_kernel(h_ref, gmix_ref, win_ref, gv_ref, ws_ref, bs_ref, mkT_ref, mv_ref, wout_ref,
                    *refs, chunk, emit_v):
    out_ref = refs[0]
    z_ref, cat_ref = refs[-MEM_HEADS - 2], refs[-1]
    ms_refs = refs[-MEM_HEADS - 1:-1]
    x = h_ref[0]
    xn = _rms(x, gmix_ref[...]).astype(BF16)
    z_ref[...] = jnp.dot(xn, win_ref[...], preferred_element_type=F32)
    stages = _mem_stages(z_ref, 2 * TOK_WIDTH, mkT_ref, mv_ref, ms_refs, cat_ref, 0)
    for score, _ in stages:
        score()
    uv = jax.nn.gelu(z_ref[:, :2 * TOK_WIDTH])
    u = uv[:, :TOK_WIDTH]
    v = _rms(uv[:, TOK_WIDTH:], gv_ref[...])
    if emit_v:
        refs[1][0] = v
    cat_ref[:, :TOK_WIDTH] = _spatial_gate(u, v, ws_ref, bs_ref, chunk).astype(BF16)
    for _, finish in stages:
        finish()
    out_ref[0] = x + jnp.dot(cat_ref[...], wout_ref[...], preferred_element_type=F32)


def _window(ref_a, ref_b, axis, other, lo, hi):
    na = ref_a.shape[1 + axis]
    parts = []
    if lo < na:
        sl = slice(lo, min(hi, na))
        parts.append(ref_a[0, sl, other] if axis == 0 else ref_a[0, other, sl])
    if hi > na:
        sl = slice(max(lo - na, 0), hi - na)
        parts.append(ref_b[0, sl, other] if axis == 0 else ref_b[0, other, sl])
    return parts[0] if len(parts) == 1 else jnp.concatenate(parts, axis=axis)


def _band_stages(z_ref, kT_refs, v_refs, bias_ref, blocks, first_tile, s_refs, m_refs, cat_ref):
    qb, kb = bias_ref.shape[2], bias_ref.shape[3]
    lane_q = lax.broadcasted_iota(jnp.int32, (qb, LANES), 1)
    low = lane_q < B_HEAD_DIM
    lane_k = lax.broadcasted_iota(jnp.int32, (kb, LANES), 1)
    low_k = lane_k < B_HEAD_DIM
    stages = []
    held = {}
    for (row0, key0, first_variant) in blocks:
        variant = jnp.where(first_tile, first_variant, 0) if first_variant else 0
        for pr in range(B_HEADS // 2):
            psl = slice(pr * LANES, (pr + 1) * LANES)
            for half in range(2):
                s_ref = s_refs[len(stages) % len(s_refs)]
                m_ref = m_refs[len(stages) % len(m_refs)]

                def score(row0=row0, key0=key0, variant=variant, psl=psl, pr=pr, half=half,
                          s_ref=s_ref, m_ref=m_ref):
                    qp = z_ref[row0:row0 + qb, psl] * (B_HEAD_DIM ** -0.5)
                    qh = (jnp.where(low, qp, 0.0) if half == 0 else jnp.where(low, 0.0, qp)).astype(BF16)
                    s = jnp.dot(qh, _window(*kT_refs, 1, psl, key0, key0 + kb), preferred_element_type=F32)
                    s = s + bias_ref[variant, 2 * pr + half]
                    s_ref[...] = s
                    m = s[:, :LANES]
                    for j in range(1, kb // LANES):
                        m = jnp.maximum(m, s[:, j * LANES:(j + 1) * LANES])
                    m_ref[...] = m

                def finish(row0=row0, key0=key0, psl=psl, half=half, s_ref=s_ref, m_ref=m_ref):
                    m = jnp.max(m_ref[...], axis=-1, keepdims=True)
                    p = jnp.exp(s_ref[...] - m).astype(BF16)
                    vv = _window(*v_refs, 0, psl, key0, key0 + kb)
                    vv = jnp.where(low_k, vv, 1.0) if half == 0 else jnp.where(low_k, 1.0, vv)
                    o = jnp.dot(p, vv.astype(BF16), preferred_element_type=F32)
                    o = o * (1.0 / pltpu.roll(o, B_HEAD_DIM, 1))
                    if half == 0:
                        held[(row0, psl.start)] = o
                    else:
                        pair = jnp.where(low, held.pop((row0, psl.start)), o)
                        cat_ref[row0:row0 + qb, psl] = pair.astype(BF16)

                stages.append((score, finish))
    return stages


def _mixer_b_kernel(h_ref, gmix_ref, win_ref, kTa_ref, kTb_ref, va_ref, vb_ref, bias_ref,
                    mkT_ref, mv_ref, wout_ref, out_ref, z_ref, *scratch, blocks):
    s_refs, m_refs, ms_refs = (scratch[k * ATT_SLOTS:(k + 1) * ATT_SLOTS] for k in range(3))
    cat_ref = scratch[-1]
    x = h_ref[0]
    xn = _rms(x, gmix_ref[...]).astype(BF16)
    z_ref[...] = jnp.dot(xn, win_ref[...], preferred_element_type=F32)
    stages = _band_stages(z_ref, (kTa_ref, kTb_ref), (va_ref, vb_ref), bias_ref, blocks, pl.program_id(1) == 0, s_refs, m_refs, cat_ref)
    stages += _mem_stages(z_ref, TOK_WIDTH, mkT_ref, mv_ref, ms_refs, cat_ref, len(stages))
    _run_skewed(stages, ATT_AHEAD)
    out_ref[0] = x + jnp.dot(cat_ref[...], wout_ref[...], preferred_element_type=F32)


def _band_stages_t(qT_ref, k_refs, vT_refs, biasT_ref, blocks, first_tile, s_refs, m_refs, cat_ref):
    kb, qb = biasT_ref.shape[2], biasT_ref.shape[3]
    top_v = lax.broadcasted_iota(jnp.int32, (LANES, kb), 0) < B_HEAD_DIM
    zeros = jnp.zeros((B_HEAD_DIM, qb), F32)
    stages = []
    held = {}
    for (row0, key0, first_variant) in blocks:
        variant = jnp.where(first_tile, first_variant, 0) if first_variant else 0
        for pr in range(B_HEADS // 2):
            psl = slice(pr * LANES, (pr + 1) * LANES)
            for half in range(2):
                s_ref = s_refs[len(stages) % len(s_refs)]
                m_ref = m_refs[len(stages) % len(m_refs)]

                def score(row0=row0, key0=key0, variant=variant, psl=psl, pr=pr, half=half,
                          s_ref=s_ref, m_ref=m_ref):
                    qp = qT_ref[psl, row0:row0 + qb]
                    qh = jnp.concatenate([qp[:B_HEAD_DIM], zeros] if half == 0 else [zeros, qp[B_HEAD_DIM:]], axis=0)
                    s = jnp.dot(_window(*k_refs, 0, psl, key0, key0 + kb), qh.astype(BF16),
                                preferred_element_type=F32)
                    s = s + biasT_ref[variant, 2 * pr + half]
                    s_ref[...] = s
                    m_ref[...] = jnp.max(s.reshape(kb // SUBLANES, SUBLANES, qb), axis=0)

                def finish(row0=row0, key0=key0, psl=psl, half=half, s_ref=s_ref, m_ref=m_ref):
                    m = jnp.max(m_ref[...], axis=0, keepdims=True)
                    p = jnp.exp(s_ref[...] - m).astype(BF16)
                    vT = _window(*vT_refs, 1, psl, key0, key0 + kb)
                    vT = jnp.where(top_v, vT, 1.0) if half == 0 else jnp.where(top_v, 1.0, vT)
                    o = jnp.dot(vT.astype(BF16), p, preferred_element_type=F32)
                    if half == 0:
                        held[(row0, psl.start)] = o[:B_HEAD_DIM] * (1.0 / o[B_HEAD_DIM:B_HEAD_DIM + 1])
                    else:
                        on = o[B_HEAD_DIM:] * (1.0 / o[0:1])
                        pair_t = jnp.concatenate([held.pop((row0, psl.start)), on], axis=0)
                        cat_ref[row0:row0 + qb, psl] = pair_t.T.astype(BF16)

                stages.append((score, finish))
    return stages


def _mixer_bt_kernel(h_ref, gmix_ref, wqT_ref, wmem_ref, ka_ref, kb_ref, vTa_ref, vTb_ref, biasT_ref,
                     mkT_ref, mv_ref, wout_ref, out_ref, qT_ref, z_ref, *scratch, blocks):
    s_refs, m_refs, ms_refs = (scratch[k * ATT_SLOTS:(k + 1) * ATT_SLOTS] for k in range(3))
    cat_ref = scratch[-1]
    x = h_ref[0]
    xn = _rms(x, gmix_ref[...]).astype(BF16)
    qT = lax.dot_general(wqT_ref[...], xn, (((1,), (1,)), ((), ())), preferred_element_type=F32)
    qT_ref[...] = qT * (B_HEAD_DIM ** -0.5)
    z_ref[...] = jnp.dot(xn, wmem_ref[...], preferred_element_type=F32)
    stages = _band_stages_t(qT_ref, (ka_ref, kb_ref), (vTa_ref, vTb_ref), biasT_ref, blocks,
                            pl.program_id(1) == 0, s_refs, m_refs, cat_ref)
    stages += _mem_stages(z_ref, 0, mkT_ref, mv_ref, ms_refs, cat_ref, len(stages))
    _run_skewed(stages, ATT_AHEAD)
    out_ref[0] = x + jnp.dot(cat_ref[...], wout_ref[...], preferred_element_type=F32)


def _mixer_bt_kv_kernel(h_ref, gmix_ref, wqT_ref, wmem_ref, gkv_ref, wkv_ref, biasT_ref, mkT_ref, mv_ref,
                        wout_ref, out_ref, kop_ref, vop_ref, kf_ref, vf_ref,
                        qT_ref, z_ref, kprev_ref, kcur_ref, vprev_ref, vcur_ref, *scratch, blocks):
    s_refs, m_refs, ms_refs = (scratch[k * ATT_SLOTS:(k + 1) * ATT_SLOTS] for k in range(3))
    cat_ref = scratch[-1]

    @pl.when(pl.program_id(1) == 0)
    def _():
        kprev_ref[...] = jnp.zeros_like(kprev_ref)
        vprev_ref[...] = jnp.zeros_like(vprev_ref)

    @pl.when(pl.program_id(1) > 0)
    def _():
        kprev_ref[...] = kcur_ref[...]
        vprev_ref[...] = vcur_ref[...]

    x = h_ref[0]
    kv = jnp.dot(_rms(x, gkv_ref[...]).astype(BF16), wkv_ref[...], preferred_element_type=F32)
    k, v = kv[:, :TOK_WIDTH], kv[:, TOK_WIDTH:]
    kf_ref[0] = k
    vf_ref[0] = v
    kcur_ref[0] = k.astype(BF16)
    vcur_ref[0] = v.T.astype(BF16)
    xn = _rms(x, gmix_ref[...]).astype(BF16)
    qT = lax.dot_general(wqT_ref[...], xn, (((1,), (1,)), ((), ())), preferred_element_type=F32)
    qT_ref[...] = qT * (B_HEAD_DIM ** -0.5)
    z_ref[...] = jnp.dot(xn, wmem_ref[...], preferred_element_type=F32)
    stages = _band_stages_t(qT_ref, (kprev_ref, kcur_ref), (vprev_ref, vcur_ref), biasT_ref, blocks,
                            pl.program_id(1) == 0, s_refs, m_refs, cat_ref)
    stages += _mem_stages(z_ref, 0, mkT_ref, mv_ref, ms_refs, cat_ref, len(stages))
    _run_skewed(stages, ATT_AHEAD)
    out_ref[0] = x + jnp.dot(cat_ref[...], wout_ref[...], preferred_element_type=F32)
    kop_ref[...] = kcur_ref[...]
    vop_ref[...] = vcur_ref[...]


def _ffn_kernel(h_ref, g_ref, wup_ref, cw_ref, cb_ref, wd_ref, prev_ref, gfin_ref,
                out_ref, state_ref, carry_ref, xn_ref, a_ref, gate_ref, act_ref, *, final_norm):
    @pl.when(pl.program_id(1) == 0)
    def _():
        carry_ref[...] = prev_ref[...]

    n_seq, seq_len, _ = h_ref.shape
    rows = n_seq * seq_len
    x = h_ref[...].reshape(rows, D_MODEL)
    xn_ref[...] = _rms(x, g_ref[...]).astype(BF16)
    head_row = lax.broadcasted_iota(jnp.int32, (SUBLANES, FF_CHUNK), 0)

    for f in range(N_FF_CHUNKS):
        slot = f % FF_SLOTS
        fsl = slice(f * FF_CHUNK, (f + 1) * FF_CHUNK)
        a_ref[slot] = jnp.dot(xn_ref[...], wup_ref[:, fsl], preferred_element_type=F32)
        gate_ref[slot] = jnp.dot(xn_ref[...], wup_ref[:, D_FF + f * FF_CHUNK:D_FF + (f + 1) * FF_CHUNK],
                                 preferred_element_type=F32)
        a = a_ref[slot]
        gate = gate_ref[slot]
        w0, w1, w2 = cw_ref[0:1, fsl], cw_ref[1:2, fsl], cw_ref[2:3, fsl]
        ys = []
        for s in range(n_seq):
            a_s = a[s * seq_len:(s + 1) * seq_len]
            p0, p1 = carry_ref[s, 0:1, fsl], carry_ref[s, 1:2, fsl]
            r1, r2 = pltpu.roll(a_s, 1, 0), pltpu.roll(a_s, 2, 0)
            a1 = jnp.concatenate([jnp.where(head_row == 0, p1, r1[:SUBLANES]), r1[SUBLANES:]], axis=0)
            a2 = jnp.concatenate([jnp.where(head_row == 0, p0, jnp.where(head_row == 1, p1, r2[:SUBLANES])),
                                  r2[SUBLANES:]], axis=0)
            carry_ref[s, :, fsl] = a_s[seq_len - 2:seq_len]
            ys.append(cb_ref[:, fsl] + w0 * a2 + w1 * a1 + w2 * a_s)
        y = jnp.concatenate(ys, axis=0) if n_seq > 1 else ys[0]
        act_ref[:, fsl] = (jax.nn.gelu(y) * gate).astype(BF16)

    state_ref[...] = carry_ref[...]
    hn = x
    for lo, hi in ((0, FF_SPLIT), (FF_SPLIT, D_FF)):
        hn = hn + jnp.dot(act_ref[:, lo:hi], wd_ref[lo:hi, :], preferred_element_type=F32)
    out = _rms(hn, gfin_ref[...]) if final_norm else hn
    out_ref[...] = out.reshape(n_seq, seq_len, D_MODEL)


def _kv_kernel(h_ref, g_ref, w_ref, kT_ref, v_ref, kf_ref, vf_ref):
    hn = _rms(h_ref[0], g_ref[...]).astype(BF16)
    kv = jnp.dot(hn, w_ref[...], preferred_element_type=F32)
    k, v = kv[:, :TOK_WIDTH], kv[:, TOK_WIDTH:]
    kf_ref[0] = k
    vf_ref[0] = v
    kT_ref[0] = k.T.astype(BF16)
    v_ref[0] = v.astype(BF16)


def _mem_kv_kernel(m_ref, g_ref, w_ref, k_ref, v_ref, kT_ref, vb_ref):
    n = m_ref.shape[0]
    mn = _rms(m_ref[...].reshape(n * N_MEM, D_MODEL), g_ref[0]).astype(BF16)
    kv = jnp.dot(mn, w_ref[0], preferred_element_type=F32)
    for s in range(n):
        k, v = kv[s * N_MEM:(s + 1) * N_MEM, :MEM_WIDTH], kv[s * N_MEM:(s + 1) * N_MEM, MEM_WIDTH:]
        k_ref[0, s] = k.reshape(N_MEM, MEM_HEADS, MEM_HEAD_DIM)
        v_ref[0, s] = v.reshape(N_MEM, MEM_HEADS, MEM_HEAD_DIM)
        kT_ref[0, s] = k.T.astype(BF16)
        vb_ref[0, s] = v.astype(BF16)


def _const_spec(shape):
    nd = len(shape)
    return pl.BlockSpec(shape, lambda b, i: (0,) * nd, pipeline_mode=pl.Buffered(1))


def _layer_spec(stacked, l):
    nd = stacked.ndim
    return pl.BlockSpec((None,) + stacked.shape[1:], lambda b, i: (l,) + (0,) * (nd - 1),
                        pipeline_mode=pl.Buffered(1))


def _mem_specs(l):
    return [pl.BlockSpec((None, 1, MEM_WIDTH, N_MEM), lambda b, i: (l, b, 0, 0)),
            pl.BlockSpec((None, 1, N_MEM, MEM_WIDTH), lambda b, i: (l, b, 0, 0))]


def _params():
    return pltpu.CompilerParams(dimension_semantics=("arbitrary", "arbitrary"),
                                vmem_limit_bytes=VMEM_LIMIT)


def _mixer_a(h, tile, l, wts, mk, mv, emit_v):
    nb, seq, _ = h.shape
    chunk = min(A_CHUNK, tile)
    act_spec = pl.BlockSpec((1, tile, D_MODEL), lambda b, i: (b, i, 0))
    out_shape = [jax.ShapeDtypeStruct(h.shape, F32)]
    out_specs = [act_spec]
    if emit_v:
        out_shape.append(jax.ShapeDtypeStruct((nb, seq, TOK_WIDTH), F32))
        out_specs.append(pl.BlockSpec((1, tile, TOK_WIDTH), lambda b, i: (b, i, 0)))
    stacked = [wts[k] for k in ("g_mix", "w_in_a", "g_v", "w_s", "bs_full")]
    res = pl.pallas_call(
        functools.partial(_mixer_a_kernel, chunk=chunk, emit_v=emit_v),
        grid=(nb, seq // tile),
        in_specs=[act_spec] + [_layer_spec(w, l) for w in stacked] + _mem_specs(l)
        + [_layer_spec(wts["w_out"], l)],
        out_specs=out_specs, out_shape=out_shape,
        scratch_shapes=[pltpu.VMEM((tile, 2 * TOK_WIDTH + MEM_WIDTH), F32)]
        + [pltpu.VMEM((tile, N_MEM), F32)] * MEM_HEADS + [pltpu.VMEM((tile, D_MODEL), BF16)],
        compiler_params=_params(), name="mixer_a",
    )(h, *stacked, mk, mv, wts["w_out"])
    return res if emit_v else (res[0], None)


def _mixer_b(h, tile, l, wts, kT_a, kT_b, v_a, v_b, bias, blocks, mk, mv):
    nb, seq, _ = h.shape
    j = l - N_A_LAYERS
    act_spec = pl.BlockSpec((1, tile, D_MODEL), lambda b, i: (b, i, 0))
    ka, kb, va, vb = (pl.BlockSpec((1,) + a.shape[1:], lambda b, i: (b, 0, 0)) for a in (kT_a, kT_b, v_a, v_b))
    return pl.pallas_call(
        functools.partial(_mixer_b_kernel, blocks=blocks),
        grid=(nb, seq // tile),
        in_specs=[act_spec, _layer_spec(wts["g_mix"], l), _layer_spec(wts["w_in_b"], j), ka, kb, va, vb,
                  _layer_spec(bias, j)] + _mem_specs(l) + [_layer_spec(wts["w_out"], l)],
        out_specs=act_spec, out_shape=jax.ShapeDtypeStruct(h.shape, F32),
        scratch_shapes=[pltpu.VMEM((tile, TOK_WIDTH + MEM_WIDTH), F32)]
        + [pltpu.VMEM(bias.shape[3:], F32)] * ATT_SLOTS + [pltpu.VMEM((bias.shape[3], LANES), F32)] * ATT_SLOTS
        + [pltpu.VMEM((tile, N_MEM), F32)] * ATT_SLOTS + [pltpu.VMEM((tile, D_MODEL), BF16)],
        compiler_params=_params(), name="mixer_b",
    )(h, wts["g_mix"], wts["w_in_b"], kT_a, kT_b, v_a, v_b, bias, mk, mv, wts["w_out"])


def _mixer_bt(h, tile, l, wts, k_rows, vT, blocks, mk, mv):
    nb, seq, _ = h.shape
    j = l - N_A_LAYERS
    biasT = wts["bias_prompt_t"]
    kb_keys, qb = biasT.shape[3:]
    act_spec = pl.BlockSpec((1, tile, D_MODEL), lambda b, i: (b, i, 0))
    ka = pl.BlockSpec((1, tile, TOK_WIDTH), lambda b, i: (b, jnp.maximum(i - 1, 0), 0))
    kb = pl.BlockSpec((1, tile, TOK_WIDTH), lambda b, i: (b, i, 0))
    va = pl.BlockSpec((1, TOK_WIDTH, tile), lambda b, i: (b, 0, jnp.maximum(i - 1, 0)))
    vb = pl.BlockSpec((1, TOK_WIDTH, tile), lambda b, i: (b, 0, i))
    return pl.pallas_call(
        functools.partial(_mixer_bt_kernel, blocks=blocks),
        grid=(nb, seq // tile),
        in_specs=[act_spec, _layer_spec(wts["g_mix"], l), _layer_spec(wts["w_q_t"], j),
                  _layer_spec(wts["w_in_b_mem"], j), ka, kb, va, vb, _layer_spec(biasT, j)]
        + _mem_specs(l) + [_layer_spec(wts["w_out"], l)],
        out_specs=act_spec, out_shape=jax.ShapeDtypeStruct(h.shape, F32),
        scratch_shapes=[pltpu.VMEM((TOK_WIDTH, tile), F32), pltpu.VMEM((tile, MEM_WIDTH), F32)]
        + [pltpu.VMEM((kb_keys, qb), F32)] * ATT_SLOTS + [pltpu.VMEM((SUBLANES, qb), F32)] * ATT_SLOTS
        + [pltpu.VMEM((tile, N_MEM), F32)] * ATT_SLOTS + [pltpu.VMEM((tile, D_MODEL), BF16)],
        compiler_params=_params(), name="mixer_b",
    )(h, wts["g_mix"], wts["w_q_t"], wts["w_in_b_mem"], k_rows, k_rows, vT, vT, biasT, mk, mv, wts["w_out"])


def _mixer_bt_kv(h, tile, l, wts, blocks, mk, mv):
    nb, seq, _ = h.shape
    j = l - N_A_LAYERS
    biasT = wts["bias_prompt_t"]
    kb_keys, qb = biasT.shape[3:]
    act_spec = pl.BlockSpec((1, tile, D_MODEL), lambda b, i: (b, i, 0))
    rows_spec = pl.BlockSpec((1, tile, TOK_WIDTH), lambda b, i: (b, i, 0))
    cols_spec = pl.BlockSpec((1, TOK_WIDTH, tile), lambda b, i: (b, 0, i))
    last_spec = pl.BlockSpec((1, tile, TOK_WIDTH), lambda b, i: (b, 0, 0))
    return pl.pallas_call(
        functools.partial(_mixer_bt_kv_kernel, blocks=blocks),
        grid=(nb, seq // tile),
        in_specs=[act_spec, _layer_spec(wts["g_mix"], l), _layer_spec(wts["w_q_t"], j),
                  _layer_spec(wts["w_in_b_mem"], j), _const_spec(wts["g_kv"].shape), _const_spec(wts["w_kv"].shape),
                  _layer_spec(biasT, j)] + _mem_specs(l) + [_layer_spec(wts["w_out"], l)],
        out_specs=[act_spec, rows_spec, cols_spec, last_spec, last_spec],
        out_shape=[jax.ShapeDtypeStruct(h.shape, F32),
                   jax.ShapeDtypeStruct((nb, seq, TOK_WIDTH), BF16),
                   jax.ShapeDtypeStruct((nb, TOK_WIDTH, seq), BF16),
                   jax.ShapeDtypeStruct((nb, tile, TOK_WIDTH), F32),
                   jax.ShapeDtypeStruct((nb, tile, TOK_WIDTH), F32)],
        scratch_shapes=[pltpu.VMEM((TOK_WIDTH, tile), F32), pltpu.VMEM((tile, MEM_WIDTH), F32),
                        pltpu.VMEM((1, tile, TOK_WIDTH), BF16), pltpu.VMEM((1, tile, TOK_WIDTH), BF16),
                        pltpu.VMEM((1, TOK_WIDTH, tile), BF16), pltpu.VMEM((1, TOK_WIDTH, tile), BF16)]
        + [pltpu.VMEM((kb_keys, qb), F32)] * ATT_SLOTS + [pltpu.VMEM((SUBLANES, qb), F32)] * ATT_SLOTS
        + [pltpu.VMEM((tile, N_MEM), F32)] * ATT_SLOTS + [pltpu.VMEM((tile, D_MODEL), BF16)],
        compiler_params=_params(), name="mixer_b_kv",
    )(h, wts["g_mix"], wts["w_q_t"], wts["w_in_b_mem"], wts["g_kv"], wts["w_kv"], biasT, mk, mv, wts["w_out"])


def _ffn(h, n_seq, tile, l, wts, prev, final_norm):
    nb, seq, _ = h.shape
    act_spec = pl.BlockSpec((n_seq, tile, D_MODEL), lambda b, i: (b, i, 0))
    stacked = [wts[k] for k in ("g_ffn", "w_up", "conv_w", "conv_b", "w_down")]
    return pl.pallas_call(
        functools.partial(_ffn_kernel, final_norm=final_norm),
        grid=(nb // n_seq, seq // tile),
        in_specs=[act_spec] + [_layer_spec(w, l) for w in stacked]
        + [pl.BlockSpec((None, n_seq, CONV_W - 1, D_FF), lambda b, i: (l, b, 0, 0)),
           _const_spec(wts["g_final"].shape)],
        out_specs=[act_spec, pl.BlockSpec((n_seq, CONV_W - 1, D_FF), lambda b, i: (b, 0, 0))],
        out_shape=[jax.ShapeDtypeStruct(h.shape, F32),
                   jax.ShapeDtypeStruct((nb, CONV_W - 1, D_FF), F32)],
        scratch_shapes=[pltpu.VMEM((n_seq, CONV_W - 1, D_FF), F32),
                        pltpu.VMEM((n_seq * tile, D_MODEL), BF16),
                        pltpu.VMEM((FF_SLOTS, n_seq * tile, FF_CHUNK), F32),
                        pltpu.VMEM((FF_SLOTS, n_seq * tile, FF_CHUNK), F32),
                        pltpu.VMEM((n_seq * tile, D_FF), BF16)],
        compiler_params=_params(), name="conv_ffn",
    )(h, *stacked, prev, wts["g_final"])


def _shared_kv(h, tile, g, w):
    nb, seq, _ = h.shape
    act_spec = pl.BlockSpec((1, tile, D_MODEL), lambda b, i: (b, i, 0))
    rows_spec = pl.BlockSpec((1, tile, TOK_WIDTH), lambda b, i: (b, i, 0))
    return pl.pallas_call(
        _kv_kernel,
        grid=(nb, seq // tile),
        in_specs=[act_spec, _const_spec(g.shape), _const_spec(w.shape)],
        out_specs=[pl.BlockSpec((1, TOK_WIDTH, tile), lambda b, i: (b, 0, i)), rows_spec, rows_spec, rows_spec],
        out_shape=[jax.ShapeDtypeStruct((nb, TOK_WIDTH, seq), BF16),
                   jax.ShapeDtypeStruct((nb, seq, TOK_WIDTH), BF16),
                   jax.ShapeDtypeStruct((nb, seq, TOK_WIDTH), F32),
                   jax.ShapeDtypeStruct((nb, seq, TOK_WIDTH), F32)],
        compiler_params=_params(), name="shared_kv",
    )(h, g, w)


def _mem_kv(mem, g, w):
    nb = mem.shape[0]
    n = MEM_KV_SEQS
    out_spec = pl.BlockSpec((1, n, N_MEM, MEM_HEADS, MEM_HEAD_DIM), lambda l, b: (l, b, 0, 0, 0))
    out_shape = jax.ShapeDtypeStruct((DEPTH, nb, N_MEM, MEM_HEADS, MEM_HEAD_DIM), F32)
    return pl.pallas_call(
        _mem_kv_kernel,
        grid=(DEPTH, nb // n),
        in_specs=[pl.BlockSpec((n, N_MEM, D_MODEL), lambda l, b: (b, 0, 0)),
                  pl.BlockSpec((1, 1, D_MODEL), lambda l, b: (l, 0, 0)),
                  pl.BlockSpec((1, D_MODEL, 2 * MEM_WIDTH), lambda l, b: (l, 0, 0))],
        out_specs=[out_spec, out_spec,
                   pl.BlockSpec((1, n, MEM_WIDTH, N_MEM), lambda l, b: (l, b, 0, 0)),
                   pl.BlockSpec((1, n, N_MEM, MEM_WIDTH), lambda l, b: (l, b, 0, 0))],
        out_shape=[out_shape, out_shape,
                   jax.ShapeDtypeStruct((DEPTH, nb, MEM_WIDTH, N_MEM), BF16),
                   jax.ShapeDtypeStruct((DEPTH, nb, N_MEM, MEM_WIDTH), BF16)],
        compiler_params=_params(), name="mem_kv",
    )(mem, g, w)


def _bias_kernel(e_ref, out_ref, *, keys_on_rows, first_tile_prev):
    rows, cols = out_ref.shape[-2:]
    e_rows = jnp.broadcast_to(e_ref[0, 0], (rows, e_ref.shape[-1]))
    toep = pltpu.roll(e_rows, 0, 1, stride=1, stride_axis=0)[:, :cols]
    r = lax.broadcasted_iota(jnp.int32, (rows, cols), 0)
    c = lax.broadcasted_iota(jnp.int32, (rows, cols), 1)
    k_pos, q_pos = (r, c) if keys_on_rows else (c, r)
    dc = (k_pos >> CHUNK_SHIFT) - (q_pos >> CHUNK_SHIFT)
    valid = jnp.logical_and(dc >= 0, dc <= B_LEFT_CHUNKS)
    out_ref[0, 0, 0] = jnp.where(valid, toep, NEG_INF)
    for v, n_prev in enumerate(first_tile_prev):
        out_ref[0, 1 + v, 0] = jnp.where(jnp.logical_and(valid, k_pos >= n_prev), toep, NEG_INF)


def _band_bias(rel_bias, q_chunks, k_chunks, first_tile_prev, keys_on_rows):
    layers, heads, _ = rel_bias.shape
    nq, nk = q_chunks * CHUNK, k_chunks * CHUNK
    rows, cols = (nk, nq) if keys_on_rows else (nq, nk)
    assert rows + cols <= BIAS_PERIOD
    d = np.arange(BIAS_PERIOD)
    d = np.where(d < cols, d, d - BIAS_PERIOD)
    q_minus_k = d if keys_on_rows else -d
    idx = np.clip(B_WINDOW + q_minus_k, -REL_CLIP, REL_CLIP) + REL_CLIP
    e = rel_bias.astype(F32)[..., idx].reshape(layers, heads, 1, BIAS_PERIOD)
    n_var = 1 + len(first_tile_prev)
    return pl.pallas_call(
        functools.partial(_bias_kernel, keys_on_rows=keys_on_rows, first_tile_prev=first_tile_prev),
        grid=(layers, heads),
        in_specs=[pl.BlockSpec((1, 1, 1, BIAS_PERIOD), lambda l, h: (l, h, 0, 0))],
        out_specs=pl.BlockSpec((1, n_var, 1, rows, cols), lambda l, h: (l, 0, h, 0, 0)),
        out_shape=jax.ShapeDtypeStruct((layers, n_var, heads, rows, cols), F32),
        compiler_params=_params(), name="band_bias",
    )(e)


def _trunk(h, tile, prompt, mk, mv, conv_prev, cache_kT, cache_v, wts):
    a_v, states = [], []
    k_last = v_last = None
    ffn_tile = FFN_TILE if prompt else tile
    ffn_seqs = 1 if prompt else FFN_SAMPLE_ROWS // tile
    wide_tile = WIDE_TILE if prompt else tile
    for l in range(DEPTH):
        if l < N_A_LAYERS:
            h, v_rows = _mixer_a(h, wide_tile, l, wts, mk, mv, emit_v=not prompt)
            if not prompt:
                a_v.append(v_rows)
        else:
            if prompt:
                blocks = tuple((qb * Q_BLOCK, qb * Q_BLOCK, 1 + qb) for qb in range(tile // Q_BLOCK))
                if l == N_A_LAYERS:
                    h, k_op, v_op, k_last, v_last = _mixer_bt_kv(h, tile, l, wts, blocks, mk, mv)
                else:
                    h = _mixer_bt(h, tile, l, wts, k_op, v_op, blocks, mk, mv)
            else:
                if l == N_A_LAYERS:
                    k_op, v_op, k_last, v_last = _shared_kv(h, tile, wts["g_kv"], wts["w_kv"])
                h = _mixer_b(h, tile, l, wts, cache_kT, k_op, cache_v, v_op, wts["bias_sample"], ((0, 0, 0),),
                             mk, mv)
        h, st = _ffn(h, ffn_seqs, ffn_tile, l, wts, conv_prev, final_norm=(l == DEPTH - 1))
        states.append(st)
    return h, a_v, k_last, v_last, jnp.stack(states)


def kernel(x_prompt, x_sample, mem_prompt, cache_mem_k, cache_mem_v, cache_b_k, cache_b_v, state_conv,
           g_mix, w_in_a, g_v, w_s, b_s, g_kv, w_kv, w_in_b, rel_bias, g_mem, w_mem_kv,
           w_out, g_ffn, w_up, conv_w, conv_b, w_down, g_final):
    bsz = x_prompt.shape[0]
    dbs = x_sample.shape[0]
    dec_seq = x_sample.shape[1]
    wts = {
        "g_mix": g_mix.reshape(DEPTH, 1, D_MODEL),
        "w_in_a": w_in_a.astype(BF16),
        "g_v": g_v.reshape(N_A_LAYERS, 1, TOK_WIDTH),
        "w_s": w_s,
        "bs_full": jnp.repeat(b_s.transpose(0, 2, 1), TOK_WIDTH // A_GROUPS, axis=-1),
        "g_kv": g_kv.reshape(1, D_MODEL),
        "w_kv": w_kv.astype(BF16),
        "w_in_b": w_in_b.astype(BF16),
        "w_q_t": w_in_b[:, :, :TOK_WIDTH].transpose(0, 2, 1).astype(BF16),
        "w_in_b_mem": w_in_b[:, :, TOK_WIDTH:].astype(BF16),
        "bias_prompt_t": _band_bias(rel_bias, Q_BLOCK // CHUNK, Q_BLOCK // CHUNK + B_LEFT_CHUNKS,
                                    tuple(B_WINDOW - qb * Q_BLOCK for qb in range(PROMPT_TILE // Q_BLOCK)), True),
        "bias_sample": _band_bias(rel_bias, 1, 1 + B_LEFT_CHUNKS, (), False),
        "w_out": w_out.astype(BF16),
        "g_ffn": g_ffn.reshape(DEPTH, 1, D_MODEL),
        "w_up": w_up.astype(BF16),
        "conv_w": conv_w,
        "conv_b": conv_b.reshape(DEPTH, 1, D_FF),
        "w_down": w_down.astype(BF16),
        "g_final": g_final.reshape(1, D_MODEL),
    }

    mem_k_p, mem_v_p, mkT_p, mv_p = _mem_kv(mem_prompt, g_mem.reshape(DEPTH, 1, D_MODEL),
                                            w_mem_kv.astype(BF16))
    conv0 = jnp.zeros((DEPTH, bsz, CONV_W - 1, D_FF), F32)
    y_p, _, k_p, v_p, conv_p = _trunk(x_prompt, PROMPT_TILE, True, mkT_p, mv_p, conv0, None, None, wts)

    mkT_s = cache_mem_k.reshape(DEPTH, dbs, N_MEM, MEM_WIDTH).transpose(0, 1, 3, 2).astype(BF16)
    mv_s = cache_mem_v.reshape(DEPTH, dbs, N_MEM, MEM_WIDTH).astype(BF16)
    cache_kT = cache_b_k.reshape(dbs, -1, TOK_WIDTH).transpose(0, 2, 1).astype(BF16)
    cache_v = cache_b_v.reshape(dbs, -1, TOK_WIDTH).astype(BF16)
    y_s, a_v_s, k_s, v_s, conv_s = _trunk(x_sample, dec_seq, False, mkT_s, mv_s, state_conv,
                                          cache_kT, cache_v, wts)

    hd = (B_HEADS, B_HEAD_DIM)
    return (y_p, y_s,
            mem_k_p, mem_v_p,
            k_p.reshape(bsz, B_WINDOW, *hd), v_p.reshape(bsz, B_WINDOW, *hd),
            conv_p,
            jnp.stack(a_v_s),
            k_s.reshape(dbs, dec_seq, *hd), v_s.reshape(dbs, dec_seq, *hd),
            conv_s)
```
